```python
import jax, jax.numpy as jnp
from jax import lax
import numpy as np

D_MODEL = 4096
BATCH = 1
SEQ = 8192
DEPTH = 4

CHUNK = 64
EPS = 1e-6

MLA_HEADS = D_MODEL // 256
MLA_NOPE = 128
MLA_ROPE = 64
MLA_V = 128
Q_RANK = 1024
KV_RANK = 512
ROPE_THETA = 10000.0
ATTN_BLOCK = 128
MLA_WIDTH = MLA_HEADS * MLA_V

SG_BLOCK = 128
SG_GROUPS = 4
SG_WIDTH = D_MODEL // 4
SG_GROUP_DIM = SG_WIDTH // SG_GROUPS

ML_HEADS = 4
ML_WIDTH = D_MODEL // 4
ML_HEAD_DIM = ML_WIDTH // ML_HEADS
CONV_K = 4

N_BRANCH = 3
IN_SIZES = (Q_RANK, KV_RANK, MLA_ROPE, MLA_WIDTH,
            SG_WIDTH, SG_WIDTH, SG_WIDTH,
            ML_WIDTH, ML_WIDTH, ML_WIDTH, ML_WIDTH, ML_HEADS, ML_HEADS,
            N_BRANCH * D_MODEL)
IN_WIDTH = sum(IN_SIZES)

kernel_name = "hybrid_mla_gmlp_mlstm_gated_block"


def rmsnorm(x, g):
    xf = x.astype(jnp.float32)
    y = xf * lax.rsqrt(jnp.mean(xf * xf, axis=-1, keepdims=True) + EPS)
    return (y * g.astype(jnp.float32)).astype(x.dtype)


def layernorm(x, g, b):
    xf = x.astype(jnp.float32)
    mu = jnp.mean(xf, axis=-1, keepdims=True)
    var = jnp.mean(jnp.square(xf - mu), axis=-1, keepdims=True)
    y = (xf - mu) * lax.rsqrt(var + EPS)
    return (y * g.astype(jnp.float32) + b.astype(jnp.float32)).astype(x.dtype)


def apply_rope(x):
    S, R = x.shape[1], x.shape[-1]
    inv_freq = ROPE_THETA ** (-jnp.arange(0, R, 2, dtype=jnp.float32) / R)
    ang = jnp.arange(S, dtype=jnp.float32)[:, None] * inv_freq[None, :]
    cos = jnp.cos(ang)[None, :, None, :]
    sin = jnp.sin(ang)[None, :, None, :]
    xf = x.astype(jnp.float32)
    x1, x2 = xf[..., : R // 2], xf[..., R // 2:]
    return jnp.concatenate([x1 * cos - x2 * sin, x1 * sin + x2 * cos], axis=-1).astype(x.dtype)


def chunk_causal_attention(q, k, v):
    B, S, H, Dk = q.shape
    Dv = v.shape[-1]
    nb = S // ATTN_BLOCK
    scale = Dk ** -0.5
    qb = jnp.moveaxis(q.reshape(B, nb, ATTN_BLOCK, H, Dk), 1, 0)
    k_chunk = jnp.arange(S) // CHUNK

    def one_block(args):
        qi, bi = args
        q_chunk = (bi * ATTN_BLOCK + jnp.arange(ATTN_BLOCK)) // CHUNK
        s = jnp.einsum('bqhd,bkhd->bhqk', qi, k).astype(jnp.float32) * scale
        mask = k_chunk[None, :] <= q_chunk[:, None]
        s = jnp.where(mask[None, None], s, -jnp.inf)
        p = jax.nn.softmax(s, axis=-1).astype(v.dtype)
        return jnp.einsum('bhqk,bkhd->bqhd', p, v)

    out = lax.map(one_block, (qb, jnp.arange(nb)))
    return jnp.moveaxis(out, 0, 1).reshape(B, S, H, Dv)


def mla_branch(cq, ckv, kr, g_q, g_kv, w_uq, w_ukv):
    B, S, _ = cq.shape
    H = MLA_HEADS
    q = (rmsnorm(cq, g_q) @ w_uq).reshape(B, S, H, MLA_NOPE + MLA_ROPE)
    kv = (rmsnorm(ckv, g_kv) @ w_ukv).reshape(B, S, H, MLA_NOPE + MLA_V)
    k_nope, v = kv[..., :MLA_NOPE], kv[..., MLA_NOPE:]
    q = jnp.concatenate([q[..., :MLA_NOPE], apply_rope(q[..., MLA_NOPE:])], axis=-1)
    k_rot = apply_rope(kr[:, :, None, :])
    k = jnp.concatenate([k_nope, jnp.broadcast_to(k_rot, (B, S, H, MLA_ROPE))], axis=-1)
    out = chunk_causal_attention(q, k, v)
    return out.reshape(B, S, MLA_WIDTH)


def spatial_gating_branch(u, v, ln_g, ln_b, w_s, b_s):
    B, S, _ = u.shape
    nb = S // SG_BLOCK
    vn = layernorm(v, ln_g, ln_b).reshape(B, nb, SG_BLOCK, SG_GROUPS, SG_GROUP_DIM)
    pos_chunk = jnp.arange(SG_BLOCK) // CHUNK
    mask = pos_chunk[None, :] <= pos_chunk[:, None]
    w = jnp.where(mask[None], w_s, jnp.zeros_like(w_s))
    s = jnp.einsum('gij,bnjgc->bnigc', w, vn) + jnp.transpose(b_s)[None, None, :, :, None]
    return u * s.reshape(B, S, SG_WIDTH).astype(u.dtype)


def causal_depthwise_conv(x, w, b):
    K, C = w.shape
    y = lax.conv_general_dilated(x, w[:, None, :].astype(x.dtype), window_strides=(1,),
                                 padding=[(K - 1, 0)], dimension_numbers=('NWC', 'WIO', 'NWC'),
                                 feature_group_count=C)
    return y + b


def mlstm_branch(xm, v_in, o_pre, i_pre, f_pre, conv_w, conv_b, w_q, w_k, b_i, b_f, gn_g):
    B, S, _ = xm.shape
    H, Dh, L = ML_HEADS, ML_HEAD_DIM, CHUNK
    nc = S // L
    f32 = jnp.float32
    xc = jax.nn.silu(causal_depthwise_conv(xm, conv_w, conv_b)).reshape(B, S, H, Dh)
    q = jnp.einsum('bshd,hde->bshe', xc, w_q).astype(f32)
    k = (jnp.einsum('bshd,hde->bshe', xc, w_k) * (Dh ** -0.5)).astype(f32)
    v = v_in.reshape(B, S, H, Dh).astype(f32)
    log_i = (i_pre + b_i).astype(f32)
    log_f = jax.nn.log_sigmoid((f_pre + b_f).astype(f32))

    qc = q.reshape(B, nc, L, H, Dh)
    kc = k.reshape(B, nc, L, H, Dh)
    vc = v.reshape(B, nc, L, H, Dh)
    ic = log_i.reshape(B, nc, L, H)
    g = jnp.cumsum(log_f.reshape(B, nc, L, H), axis=2)
    G = g[:, :, -1]

    w_end = G[:, :, None, :] - g + ic
    a = jnp.max(w_end, axis=2)
    e_end = jnp.exp(w_end - a[:, :, None, :])
    C_loc = jnp.einsum('bnlh,bnlhd,bnlhe->bnhde', e_end, vc, kc)
    n_loc = jnp.einsum('bnlh,bnlhe->bnhe', e_end, kc)

    def step(carry, inp):
        C, n, m = carry
        Gc, ac, Cl, nl = inp
        m_new = jnp.maximum(Gc + m, ac)
        sp = jnp.exp(Gc + m - m_new)
        sl = jnp.exp(ac - m_new)
        C_new = sp[..., None, None] * C + sl[..., None, None] * Cl
        n_new = sp[..., None] * n + sl[..., None] * nl
        return (C_new, n_new, m_new), (C, n, m)

    init = (jnp.zeros((B, H, Dh, Dh), f32), jnp.zeros((B, H, Dh), f32), jnp.zeros((B, H), f32))
    xs = (jnp.moveaxis(G, 1, 0), jnp.moveaxis(a, 1, 0), jnp.moveaxis(C_loc, 1, 0), jnp.moveaxis(n_loc, 1, 0))
    _, (C_prev, n_prev, m_prev) = lax.scan(step, init, xs)
    C_prev = jnp.moveaxis(C_prev, 0, 1)
    n_prev = jnp.moveaxis(n_prev, 0, 1)
    m_prev = jnp.moveaxis(m_prev, 0, 1)

    Dmat = g[:, :, :, None, :] - g[:, :, None, :, :] + ic[:, :, None, :, :]
    tri = jnp.tril(jnp.ones((L, L), dtype=bool))
    Dmat = jnp.where(tri[None, None, :, :, None], Dmat, -jnp.inf)
    inter_log = g + m_prev[:, :, None, :]
    m_i = jnp.maximum(jnp.max(Dmat, axis=3), inter_log)
    P = jnp.exp(Dmat - m_i[:, :, :, None, :])
    s = jnp.einsum('bnihd,bnjhd->bnijh', qc, kc) * P
    inter_scale = jnp.exp(inter_log - m_i)
    num = jnp.einsum('bnijh,bnjhd->bnihd', s, vc) + inter_scale[..., None] * jnp.einsum('bnhde,bnihe->bnihd', C_prev, qc)
    den = jnp.sum(s, axis=3) + inter_scale * jnp.einsum('bnhe,bnihe->bnih', n_prev, qc)
    h = num / jnp.maximum(jnp.abs(den), jnp.exp(-m_i))[..., None]
    h = h.reshape(B, S, H, Dh) * jax.nn.sigmoid(o_pre.astype(f32)).reshape(B, S, H, Dh)
    mu = jnp.mean(h, axis=-1, keepdims=True)
    var = jnp.mean(jnp.square(h - mu), axis=-1, keepdims=True)
    h = (h - mu) * lax.rsqrt(var + EPS)
    h = h.reshape(B, S, ML_WIDTH) * gn_g.astype(f32)
    return h.astype(xm.dtype)


def hybrid_layer(x, norm_g, w_in, mla_gq, mla_gkv, mla_wuq, mla_wukv, sg_ln_g, sg_ln_b, sg_ws, sg_bs,
                 ml_conv_w, ml_conv_b, ml_wq, ml_wk, ml_bi, ml_bf, ml_gn_g, w_pa, w_pb, w_pc, w_out):
    B, S, D = x.shape
    h = rmsnorm(x, norm_g)
    proj = h @ w_in
    split_points = np.cumsum(np.array(IN_SIZES))[:-1].tolist()
    (cq, ckv, kr, z_a, sg_u, sg_v, z_b, ml_x, ml_v, ml_o, z_c, ml_i, ml_f, gate_pre) = jnp.split(proj, split_points, axis=-1)

    y_a = mla_branch(cq, ckv, kr, mla_gq, mla_gkv, mla_wuq, mla_wukv) * jax.nn.silu(z_a)
    y_b = spatial_gating_branch(sg_u, sg_v, sg_ln_g, sg_ln_b, sg_ws, sg_bs) * jax.nn.silu(z_b)
    y_c = mlstm_branch(ml_x, ml_v, ml_o, ml_i, ml_f, ml_conv_w, ml_conv_b, ml_wq, ml_wk,
                       ml_bi, ml_bf, ml_gn_g) * jax.nn.silu(z_c)

    gates = jax.nn.sigmoid(gate_pre.astype(jnp.float32)).astype(x.dtype).reshape(B, S, N_BRANCH, D)
    merged = gates[:, :, 0] * (y_a @ w_pa) + gates[:, :, 1] * (y_b @ w_pb) + gates[:, :, 2] * (y_c @ w_pc)
    return x + (merged @ w_out).astype(x.dtype)


def setup_inputs(seed: int = 0) -> dict:
    key = jax.random.key(seed)
    ks = jax.random.split(key, 24)
    f32 = jnp.float32
    nrm = lambda k, shape, scale: jax.random.normal(k, shape, f32) * scale
    Dp = DEPTH
    b_f = jnp.linspace(3.0, 6.0, ML_HEADS, dtype=f32)[None, :] + nrm(ks[16], (Dp, ML_HEADS), 0.01)
    return {
        "x": nrm(ks[0], (BATCH, SEQ, D_MODEL), 1.0),
        "norm_g": 1.0 + nrm(ks[1], (Dp, D_MODEL), 0.02),
        "w_in": nrm(ks[2], (Dp, D_MODEL, IN_WIDTH), D_MODEL ** -0.5),
        "mla_gq": 1.0 + nrm(ks[3], (Dp, Q_RANK), 0.02),
        "mla_gkv": 1.0 + nrm(ks[4], (Dp, KV_RANK), 0.02),
        "mla_wuq": nrm(ks[5], (Dp, Q_RANK, MLA_HEADS * (MLA_NOPE + MLA_ROPE)), Q_RANK ** -0.5),
        "mla_wukv": nrm(ks[6], (Dp, KV_RANK, MLA_HEADS * (MLA_NOPE + MLA_V)), KV_RANK ** -0.5),
        "sg_ln_g": 1.0 + nrm(ks[7], (Dp, SG_WIDTH), 0.02),
        "sg_ln_b": nrm(ks[8], (Dp, SG_WIDTH), 0.02),
        "sg_ws": nrm(ks[9], (Dp, SG_GROUPS, SG_BLOCK, SG_BLOCK), SG_BLOCK ** -0.5),
        "sg_bs": 1.0 + nrm(ks[10], (Dp, SG_GROUPS, SG_BLOCK), 0.1),
        "ml_conv_w": nrm(ks[11], (Dp, CONV_K, ML_WIDTH), CONV_K ** -0.5),
        "ml_conv_b": nrm(ks[12], (Dp, ML_WIDTH), 0.02),
        "ml_wq": nrm(ks[13], (Dp, ML_HEADS, ML_HEAD_DIM, ML_HEAD_DIM), ML_HEAD_DIM ** -0.5),
        "ml_wk": nrm(ks[14], (Dp, ML_HEADS, ML_HEAD_DIM, ML_HEAD_DIM), ML_HEAD_DIM ** -0.5),
        "ml_bi": nrm(ks[15], (Dp, ML_HEADS), 0.1),
        "ml_bf": b_f,
        "ml_gn_g": 1.0 + nrm(ks[17], (Dp, ML_WIDTH), 0.02),
        "w_pa": nrm(ks[18], (Dp, MLA_WIDTH, D_MODEL), MLA_WIDTH ** -0.5),
        "w_pb": nrm(ks[19], (Dp, SG_WIDTH, D_MODEL), SG_WIDTH ** -0.5),
        "w_pc": nrm(ks[20], (Dp, ML_WIDTH, D_MODEL), ML_WIDTH ** -0.5),
        "w_out": nrm(ks[21], (Dp, D_MODEL, D_MODEL), D_MODEL ** -0.5),
        "final_g": 1.0 + nrm(ks[22], (D_MODEL,), 0.02),
    }


def reference(x, norm_g, w_in, mla_gq, mla_gkv, mla_wuq, mla_wukv, sg_ln_g, sg_ln_b, sg_ws, sg_bs,
              ml_conv_w, ml_conv_b, ml_wq, ml_wk, ml_bi, ml_bf, ml_gn_g, w_pa, w_pb, w_pc, w_out, final_g):
    for l in range(DEPTH):
        x = hybrid_layer(x, norm_g[l], w_in[l], mla_gq[l], mla_gkv[l], mla_wuq[l], mla_wukv[l],
                         sg_ln_g[l], sg_ln_b[l], sg_ws[l], sg_bs[l],
                         ml_conv_w[l], ml_conv_b[l], ml_wq[l], ml_wk[l], ml_bi[l], ml_bf[l], ml_gn_g[l],
                         w_pa[l], w_pb[l], w_pc[l], w_out[l])
    return rmsnorm(x, final_g)
```

```python
import functools

import jax
import jax.numpy as jnp
from jax import lax
from jax.experimental import pallas as pl
from jax.experimental.pallas import tpu as pltpu

F32 = jnp.float32
BF16 = jnp.bfloat16

D_MODEL = 4096
CHUNK = 64
EPS = 1e-6

MLA_HEADS = 16
MLA_NOPE = 128
MLA_ROPE = 64
MLA_V = 128
Q_RANK = 1024
KV_RANK = 512
ROPE_THETA = 10000.0
MLA_WIDTH = MLA_HEADS * MLA_V
MLA_QK_PAD = 256

SG_BLOCK = 128
SG_GROUPS = 4
SG_WIDTH = 1024
SG_GROUP_DIM = SG_WIDTH // SG_GROUPS

ML_HEADS = 4
ML_WIDTH = 1024
ML_HEAD_DIM = 256
CONV_K = 4
ML_CHUNK = 128

LANES = 128
NEG = -1e30

COL_CQ = 0
COL_ZA = 1024
COL_SGU = 3072
COL_SGV = 4096
COL_ZB = 5120
COL_MLX = 6144
COL_MLV = 7168
COL_MLO = 8192
COL_ZC = 9216
COL_GATE = 10240
COL_CKV = 22528
COL_KR = 23040
COL_MLI = 23168
COL_MLF = 23296
IN_PAD = 23552

VMEM_LIMIT = 56 * 1024 * 1024


def _cparams(n_axes, vmem=VMEM_LIMIT):
    return pltpu.CompilerParams(dimension_semantics=("arbitrary",) * n_axes,
                                vmem_limit_bytes=vmem)


def _sigmoid(x):
    return 1.0 / (1.0 + jnp.exp(-x))


def _silu(x):
    return x * _sigmoid(x)


def _rmsnorm_body(x_ref, g_ref, o_ref):
    x = x_ref[...]
    y = x * lax.rsqrt(jnp.mean(x * x, axis=-1, keepdims=True) + EPS)
    o_ref[...] = (y * g_ref[...]).astype(o_ref.dtype)


def _rmsnorm(x, g, out_dtype):
    S, D = x.shape
    tm = min(256, S)
    return pl.pallas_call(
        _rmsnorm_body,
        grid=(S // tm,),
        in_specs=[pl.BlockSpec((tm, D), lambda i: (i, 0)),
                  pl.BlockSpec((1, D), lambda i: (0, 0))],
        out_specs=pl.BlockSpec((tm, D), lambda i: (i, 0)),
        out_shape=jax.ShapeDtypeStruct((S, D), out_dtype),
        compiler_params=_cparams(1),
        name="rmsnorm",
    )(x, g.reshape(1, D))


def _matmul_body(a_ref, b_ref, o_ref):
    o_ref[...] = jnp.dot(a_ref[...], b_ref[...], preferred_element_type=F32).astype(o_ref.dtype)


def _in_proj(h, w):
    S, K = h.shape
    N = w.shape[1]
    tm = min(1024, S)
    tn = 1024
    return pl.pallas_call(
        _matmul_body,
        grid=(N // tn, S // tm),
        in_specs=[pl.BlockSpec((tm, K), lambda j, i: (i, 0)),
                  pl.BlockSpec((K, tn), lambda j, i: (0, j))],
        out_specs=pl.BlockSpec((tm, tn), lambda j, i: (i, j)),
        out_shape=jax.ShapeDtypeStruct((S, N), F32),
        compiler_params=_cparams(2),
        name="in_proj",
    )(h, w)


def _rope(seg, cosf, sin_lo, sin_hi):
    return (seg * cosf
            + pltpu.roll(seg, LANES - MLA_ROPE // 2, 1) * sin_lo
            + pltpu.roll(seg, MLA_ROPE // 2, 1) * sin_hi)


def _qproj_body(cq_ref, g_ref, w_ref, cos_ref, slo_ref, shi_ref, o_ref, a_scr):
    @pl.when(pl.program_id(1) == 0)
    def _():
        c = cq_ref[...]
        y = c * lax.rsqrt(jnp.mean(c * c, axis=-1, keepdims=True) + EPS)
        a_scr[...] = (y * g_ref[...]).astype(BF16)

    res = jnp.dot(a_scr[...], w_ref[...], preferred_element_type=F32)
    cosf, slo, shi = cos_ref[...], slo_ref[...], shi_ref[...]
    for hh in range(o_ref.shape[0]):
        base = hh * MLA_QK_PAD
        o_ref[hh, :, 0:MLA_NOPE] = res[:, base:base + MLA_NOPE].astype(BF16)
        o_ref[hh, :, MLA_NOPE:MLA_QK_PAD] = _rope(
            res[:, base + MLA_NOPE:base + MLA_QK_PAD], cosf, slo, shi).astype(BF16)


def _q_proj(proj, g, w, cosf, slo, shi):
    S = proj.shape[0]
    tm = min(1024, S)
    hpb = 4
    tn = hpb * MLA_QK_PAD
    row = lambda i, j: (i, 0)
    return pl.pallas_call(
        _qproj_body,
        grid=(S // tm, MLA_HEADS // hpb),
        in_specs=[pl.BlockSpec((tm, Q_RANK), lambda i, j: (i, COL_CQ // Q_RANK)),
                  pl.BlockSpec((1, Q_RANK), lambda i, j: (0, 0)),
                  pl.BlockSpec((Q_RANK, tn), lambda i, j: (0, j)),
                  pl.BlockSpec((tm, LANES), row),
                  pl.BlockSpec((tm, LANES), row),
                  pl.BlockSpec((tm, LANES), row)],
        out_specs=pl.BlockSpec((hpb, tm, MLA_QK_PAD), lambda i, j: (j, i, 0)),
        out_shape=jax.ShapeDtypeStruct((MLA_HEADS, S, MLA_QK_PAD), BF16),
        scratch_shapes=[pltpu.VMEM((tm, Q_RANK), BF16)],
        compiler_params=_cparams(2),
        name="mla_q_proj",
    )(proj, g.reshape(1, Q_RANK), w, cosf, slo, shi)


def _kvproj_body(ckv_ref, kr_ref, g_ref, w_ref, cos_ref, slo_ref, shi_ref, k_ref, v_ref, a_scr):
    @pl.when(pl.program_id(1) == 0)
    def _():
        c = ckv_ref[...]
        y = c * lax.rsqrt(jnp.mean(c * c, axis=-1, keepdims=True) + EPS)
        a_scr[...] = (y * g_ref[...]).astype(BF16)

    res = jnp.dot(a_scr[...], w_ref[...], preferred_element_type=F32)
    k_rot = _rope(kr_ref[...], cos_ref[...], slo_ref[...], shi_ref[...]).astype(BF16)
    width = MLA_NOPE + MLA_V
    for hh in range(k_ref.shape[0]):
        base = hh * width
        k_ref[hh, :, 0:MLA_NOPE] = res[:, base:base + MLA_NOPE].astype(BF16)
        k_ref[hh, :, MLA_NOPE:MLA_QK_PAD] = k_rot
        v_ref[hh] = res[:, base + MLA_NOPE:base + width].astype(BF16)


def _kv_proj(proj, g, w, cosf, slo, shi):
    S = proj.shape[0]
    tm = min(1024, S)
    hpb = 4
    tn = hpb * (MLA_NOPE + MLA_V)
    row = lambda i, j: (i, 0)
    return pl.pallas_call(
        _kvproj_body,
        grid=(S // tm, MLA_HEADS // hpb),
        in_specs=[pl.BlockSpec((tm, KV_RANK), lambda i, j: (i, COL_CKV // KV_RANK)),
                  pl.BlockSpec((tm, LANES), lambda i, j: (i, COL_KR // LANES)),
                  pl.BlockSpec((1, KV_RANK), lambda i, j: (0, 0)),
                  pl.BlockSpec((KV_RANK, tn), lambda i, j: (0, j)),
                  pl.BlockSpec((tm, LANES), row),
                  pl.BlockSpec((tm, LANES), row),
                  pl.BlockSpec((tm, LANES), row)],
        out_specs=[pl.BlockSpec((hpb, tm, MLA_QK_PAD), lambda i, j: (j, i, 0)),
                   pl.BlockSpec((hpb, tm, MLA_V), lambda i, j: (j, i, 0))],
        out_shape=[jax.ShapeDtypeStruct((MLA_HEADS, S, MLA_QK_PAD), BF16),
                   jax.ShapeDtypeStruct((MLA_HEADS, S, MLA_V), BF16)],
        scratch_shapes=[pltpu.VMEM((tm, KV_RANK), BF16)],
        compiler_params=_cparams(2),
        name="mla_kv_proj",
    )(proj, proj, g.reshape(1, KV_RANK), w, cosf, slo, shi)


def _attn_body(q_ref, k_ref, v_ref, z_ref, o_ref, *, blk, scale):
    i = pl.program_id(1)
    q = q_ref[0]

    def scores(kb):
        k = k_ref[0, pl.ds(pl.multiple_of(kb * blk, blk), blk), :]
        return lax.dot_general(q, k, (((1,), (1,)), ((), ())), preferred_element_type=F32) * scale

    def values(kb):
        return v_ref[0, pl.ds(pl.multiple_of(kb * blk, blk), blk), :]

    row_chunk = lax.broadcasted_iota(jnp.int32, (blk, blk), 0) // CHUNK
    col_chunk = lax.broadcasted_iota(jnp.int32, (blk, blk), 1) // CHUNK
    s = jnp.where(col_chunk <= row_chunk, scores(i), NEG)
    m = jnp.max(s, axis=-1, keepdims=True)
    p = jnp.exp(s - m)
    l = jnp.sum(p, axis=-1, keepdims=True)
    acc = jnp.dot(p.astype(BF16), values(i), preferred_element_type=F32)

    def body(kb, carry):
        m, l, acc = carry
        s = scores(kb)
        m_new = jnp.maximum(m, jnp.max(s, axis=-1, keepdims=True))
        p = jnp.exp(s - m_new)
        alpha = jnp.exp(m - m_new)
        l = alpha * l + jnp.sum(p, axis=-1, keepdims=True)
        acc = alpha * acc + jnp.dot(p.astype(BF16), values(kb), preferred_element_type=F32)
        return m_new, l, acc

    m, l, acc = lax.fori_loop(0, i, body, (m, l, acc))
    o_ref[...] = ((acc / l) * _silu(z_ref[...])).astype(o_ref.dtype)


def _attention(q, k, v, proj):
    H, S, _ = q.shape
    blk = min(512, S)
    scale = float(MLA_NOPE + MLA_ROPE) ** -0.5
    return pl.pallas_call(
        functools.partial(_attn_body, blk=blk, scale=scale),
        grid=(H, S // blk),
        in_specs=[pl.BlockSpec((1, blk, MLA_QK_PAD), lambda h, i: (h, i, 0)),
                  pl.BlockSpec((1, S, MLA_QK_PAD), lambda h, i: (h, 0, 0)),
                  pl.BlockSpec((1, S, MLA_V), lambda h, i: (h, 0, 0)),
                  pl.BlockSpec((blk, MLA_V), lambda h, i: (i, COL_ZA // MLA_V + h))],
        out_specs=pl.BlockSpec((blk, MLA_V), lambda h, i: (i, h)),
        out_shape=jax.ShapeDtypeStruct((S, MLA_WIDTH), BF16),
        compiler_params=_cparams(2),
        name="mla_attention",
    )(q, k, v, proj)


def _sgu_body(u_ref, v_ref, z_ref, lng_ref, lnb_ref, ws_ref, bst_ref, o_ref):
    v = v_ref[...]
    mu = jnp.mean(v, axis=-1, keepdims=True)
    vc = v - mu
    var = jnp.mean(vc * vc, axis=-1, keepdims=True)
    vn = (vc * lax.rsqrt(var + EPS) * lng_ref[...] + lnb_ref[...]).astype(BF16)
    out_chunk = lax.broadcasted_iota(jnp.int32, (SG_BLOCK, SG_BLOCK), 0) // CHUNK
    in_chunk = lax.broadcasted_iota(jnp.int32, (SG_BLOCK, SG_BLOCK), 1) // CHUNK
    causal = in_chunk <= out_chunk
    bst = bst_ref[...]
    for g in range(SG_GROUPS):
        w = jnp.where(causal, ws_ref[g], 0.0).astype(BF16)
        cols = slice(g * SG_GROUP_DIM, (g + 1) * SG_GROUP_DIM)
        bias = bst[:, g:g + 1]
        for b in range(u_ref.shape[0] // SG_BLOCK):
            rows = slice(b * SG_BLOCK, (b + 1) * SG_BLOCK)
            s = jnp.dot(w, vn[rows, cols], preferred_element_type=F32) + bias
            o_ref[rows, cols] = (u_ref[rows, cols] * s * _silu(z_ref[rows, cols])).astype(o_ref.dtype)


def _spatial_gating(proj, ln_g, ln_b, w_s, b_s):
    S = proj.shape[0]
    tm = min(512, S)
    blk = lambda c: pl.BlockSpec((tm, SG_WIDTH), lambda i: (i, c // SG_WIDTH))
    full = lambda shape: pl.BlockSpec(shape, lambda i: (0,) * len(shape))
    return pl.pallas_call(
        _sgu_body,
        grid=(S // tm,),
        in_specs=[blk(COL_SGU), blk(COL_SGV), blk(COL_ZB),
                  full((1, SG_WIDTH)), full((1, SG_WIDTH)),
                  full((SG_GROUPS, SG_BLOCK, SG_BLOCK)), full((SG_BLOCK, SG_GROUPS))],
        out_specs=pl.BlockSpec((tm, SG_WIDTH), lambda i: (i, 0)),
        out_shape=jax.ShapeDtypeStruct((S, SG_WIDTH), BF16),
        compiler_params=_cparams(1),
        name="spatial_gating",
    )(proj, proj, proj, ln_g.reshape(1, SG_WIDTH), ln_b.reshape(1, SG_WIDTH), w_s, jnp.transpose(b_s))


def _mlstm_body(x_ref, v_ref, og_ref, z_ref, i_ref, f_ref, cw_ref, cb_ref, wq_ref, wk_ref,
                bi_ref, bf_ref, gn_ref, o_ref, xs_scr, ct_scr, n_scr, m_scr, h_scr, *, T, L):
    halo = 8

    @pl.when(pl.program_id(0) == 0)
    def _():
        xs_scr[0:halo, :] = jnp.zeros((halo, ML_WIDTH), F32)
        ct_scr[...] = jnp.zeros_like(ct_scr)
        n_scr[...] = jnp.zeros_like(n_scr)
        m_scr[...] = jnp.zeros_like(m_scr)

    xs_scr[halo:halo + T, :] = x_ref[...]
    cw = cw_ref[...]
    xc = cb_ref[...]
    for kk in range(CONV_K):
        off = halo - (CONV_K - 1) + kk
        xc = xc + cw[kk:kk + 1, :] * xs_scr[off:off + T, :]
    xs_scr[0:halo, :] = xs_scr[T:T + halo, :]
    xcb = _silu(xc).astype(BF16)

    ig = i_ref[...] + bi_ref[...]
    fg = f_ref[...] + bf_ref[...]
    lf = jnp.minimum(fg, 0.0) - jnp.log1p(jnp.exp(-jnp.abs(fg)))
    rr = lax.broadcasted_iota(jnp.int32, (L, L), 0)
    cc = lax.broadcasted_iota(jnp.int32, (L, L), 1)
    tril = cc <= rr
    tri_f = tril.astype(F32)
    nc = T // L
    g_l, G_l, a_l, e_l, r_l = [], [], [], [], []
    for c in range(nc):
        rows = slice(c * L, (c + 1) * L)
        g_c = jnp.dot(tri_f, lf[rows], precision=lax.Precision.HIGHEST, preferred_element_type=F32)
        G_c = g_c[L - 1:L, :]
        w_end = G_c - g_c + ig[rows]
        a_c = jnp.max(w_end, axis=0, keepdims=True)
        g_l.append(g_c)
        G_l.append(G_c)
        a_l.append(a_c)
        e_l.append(jnp.exp(w_end - a_c))
        r_l.append(ig[rows] - g_c)
    r_t = jnp.transpose(jnp.concatenate(r_l, axis=0))

    q_l, k_l = [], []
    for h in range(ML_HEADS):
        cols = slice(h * ML_HEAD_DIM, (h + 1) * ML_HEAD_DIM)
        q_l.append(jnp.dot(xcb[:, cols], wq_ref[h], preferred_element_type=F32))
        k_l.append(jnp.dot(xcb[:, cols], wk_ref[h], preferred_element_type=F32) * (ML_HEAD_DIM ** -0.5))

    m_vec = m_scr[0:1, :]
    for c in range(nc):
        rows = slice(c * L, (c + 1) * L)
        g_c, G_c, a_c, e_c = g_l[c], G_l[c], a_l[c], e_l[c]
        m_new = jnp.maximum(G_c + m_vec, a_c)
        sp = jnp.exp(G_c + m_vec - m_new)
        sl = jnp.exp(a_c - m_new)
        for h in range(ML_HEADS):
            cols = slice(h * ML_HEAD_DIM, (h + 1) * ML_HEAD_DIM)
            q_c = q_l[h][rows]
            k_c = k_l[h][rows]
            v_c = v_ref[rows, cols]
            q_cb = q_c.astype(BF16)
            g_col = g_c[:, h:h + 1]
            d = jnp.where(tril, g_col + r_t[h:h + 1, rows], NEG)
            inter_log = g_col + m_vec[:, h:h + 1]
            m_i = jnp.maximum(jnp.max(d, axis=1, keepdims=True), inter_log)
            p = jnp.exp(d - m_i)
            qk = lax.dot_general(q_cb, k_c.astype(BF16), (((1,), (1,)), ((), ())),
                                 preferred_element_type=F32)
            sm = qk * p
            inter_scale = jnp.exp(inter_log - m_i)
            ct = ct_scr[h]
            n_prev = n_scr[h:h + 1, :]
            num = (jnp.dot(sm.astype(BF16), v_c.astype(BF16), preferred_element_type=F32)
                   + inter_scale * jnp.dot(q_cb, ct.astype(BF16), preferred_element_type=F32))
            den = (jnp.sum(sm, axis=1, keepdims=True)
                   + inter_scale * jnp.sum(q_c * n_prev, axis=1, keepdims=True))
            h_scr[rows, cols] = num / jnp.maximum(jnp.abs(den), jnp.exp(-m_i))

            e_col = e_c[:, h:h + 1]
            ev = (e_col * v_c).astype(BF16)
            c_loc_t = jnp.dot(jnp.transpose(k_c).astype(BF16), ev, preferred_element_type=F32)
            sp_h = sp[:, h:h + 1]
            sl_h = sl[:, h:h + 1]
            ct_scr[h] = sp_h * ct + sl_h * c_loc_t
            n_scr[h:h + 1, :] = sp_h * n_prev + sl_h * jnp.sum(e_col * k_c, axis=0, keepdims=True)
        m_vec = m_new
    m_scr[0:1, :] = m_vec

    for h in range(ML_HEADS):
        cols = slice(h * ML_HEAD_DIM, (h + 1) * ML_HEAD_DIM)
        hs = h_scr[:, cols] * _sigmoid(og_ref[:, cols])
        mu = jnp.mean(hs, axis=-1, keepdims=True)
        hc = hs - mu
        var = jnp.mean(hc * hc, axis=-1, keepdims=True)
        y = hc * lax.rsqrt(var + EPS) * gn_ref[:, cols]
        o_ref[:, cols] = (y * _silu(z_ref[:, cols])).astype(o_ref.dtype)


def _mlstm(proj, conv_w, conv_b, w_q, w_k, b_i, b_f, gn_g):
    S = proj.shape[0]
    T = min(512, S)
    L = ML_CHUNK
    wide = lambda c: pl.BlockSpec((T, ML_WIDTH), lambda i: (i, c // ML_WIDTH))
    gate = lambda c: pl.BlockSpec((T, LANES), lambda i: (i, c // LANES))
    full = lambda shape: pl.BlockSpec(shape, lambda i: (0,) * len(shape))
    pad_gate_bias = lambda b: jnp.pad(b, (0, LANES - ML_HEADS)).reshape(1, LANES)
    return pl.pallas_call(
        functools.partial(_mlstm_body, T=T, L=L),
        grid=(S // T,),
        in_specs=[wide(COL_MLX), wide(COL_MLV), wide(COL_MLO), wide(COL_ZC),
                  gate(COL_MLI), gate(COL_MLF),
                  full((CONV_K, ML_WIDTH)), full((1, ML_WIDTH)),
                  full((ML_HEADS, ML_HEAD_DIM, ML_HEAD_DIM)), full((ML_HEADS, ML_HEAD_DIM, ML_HEAD_DIM)),
                  full((1, LANES)), full((1, LANES)), full((1, ML_WIDTH))],
        out_specs=pl.BlockSpec((T, ML_WIDTH), lambda i: (i, 0)),
        out_shape=jax.ShapeDtypeStruct((S, ML_WIDTH), BF16),
        scratch_shapes=[pltpu.VMEM((T + 8, ML_WIDTH), F32),
                        pltpu.VMEM((ML_HEADS, ML_HEAD_DIM, ML_HEAD_DIM), F32),
                        pltpu.VMEM((8, ML_HEAD_DIM), F32),
                        pltpu.VMEM((8, LANES), F32),
                        pltpu.VMEM((T, ML_WIDTH), F32)],
        compiler_params=_cparams(1),
        name="mlstm",
    )(proj, proj, proj, proj, proj, proj, conv_w, conv_b.reshape(1, ML_WIDTH), w_q, w_k,
      pad_gate_bias(b_i), pad_gate_bias(b_f), gn_g.reshape(1, ML_WIDTH))


def _merge_body(ya_ref, yb_ref, yc_ref, wa_ref, wb_ref, wc_ref, g0_ref, g1_ref, g2_ref, o_ref):
    pa = jnp.dot(ya_ref[...], wa_ref[...], preferred_element_type=F32)
    pb = jnp.dot(yb_ref[...], wb_ref[...], preferred_element_type=F32)
    pc = jnp.dot(yc_ref[...], wc_ref[...], preferred_element_type=F32)
    merged = _sigmoid(g0_ref[...]) * pa + _sigmoid(g1_ref[...]) * pb + _sigmoid(g2_ref[...]) * pc
    o_ref[...] = merged.astype(o_ref.dtype)


def _merge(y_a, y_b, y_c, w_pa, w_pb, w_pc, proj):
    S = y_a.shape[0]
    D = w_pa.shape[1]
    tm = min(1024, S)
    tn = 512
    act = lambda width: pl.BlockSpec((tm, width), lambda i, j: (i, 0))
    wgt = lambda width: pl.BlockSpec((width, tn), lambda i, j: (0, j))
    gate = lambda b: pl.BlockSpec((tm, tn), lambda i, j: (i, (COL_GATE + b * D) // tn + j))
    return pl.pallas_call(
        _merge_body,
        grid=(S // tm, D // tn),
        in_specs=[act(MLA_WIDTH), act(SG_WIDTH), act(ML_WIDTH),
                  wgt(MLA_WIDTH), wgt(SG_WIDTH), wgt(ML_WIDTH),
                  gate(0), gate(1), gate(2)],
        out_specs=pl.BlockSpec((tm, tn), lambda i, j: (i, j)),
        out_shape=jax.ShapeDtypeStruct((S, D), BF16),
        compiler_params=_cparams(2),
        name="gated_merge",
    )(y_a, y_b, y_c, w_pa, w_pb, w_pc, proj, proj, proj)


def _outproj_body(m_ref, w_ref, x_ref, o_ref):
    o_ref[...] = x_ref[...] + jnp.dot(m_ref[...], w_ref[...], preferred_element_type=F32)


def _out_proj(merged, w_out, x):
    S, D = x.shape
    tm = min(512, S)
    tn = 1024
    return pl.pallas_call(
        _outproj_body,
        grid=(D // tn, S // tm),
        in_specs=[pl.BlockSpec((tm, D), lambda j, i: (i, 0)),
                  pl.BlockSpec((D, tn), lambda j, i: (0, j)),
                  pl.BlockSpec((tm, tn), lambda j, i: (i, j))],
        out_specs=pl.BlockSpec((tm, tn), lambda j, i: (i, j)),
        out_shape=jax.ShapeDtypeStruct((S, D), F32),
        compiler_params=_cparams(2),
        name="out_proj",
    )(merged, w_out, x)


def _pack_w_in(w_in):
    sizes = (Q_RANK, KV_RANK, MLA_ROPE, MLA_WIDTH, SG_WIDTH, SG_WIDTH, SG_WIDTH,
             ML_WIDTH, ML_WIDTH, ML_WIDTH, ML_WIDTH, ML_HEADS, ML_HEADS, 3 * D_MODEL)
    offs = [0]
    for s in sizes:
        offs.append(offs[-1] + s)
    seg = [w_in[..., offs[n]:offs[n + 1]].astype(BF16) for n in range(len(sizes))]
    cq, ckv, kr, z_a, sg_u, sg_v, z_b, ml_x, ml_v, ml_o, z_c, ml_i, ml_f, gate = seg
    zeros = lambda n: jnp.zeros(w_in.shape[:-1] + (n,), BF16)
    packed = jnp.concatenate(
        [cq, z_a, sg_u, sg_v, z_b, ml_x, ml_v, ml_o, z_c, gate, ckv,
         kr, zeros(LANES - MLA_ROPE), ml_i, zeros(LANES - ML_HEADS), ml_f, zeros(LANES - ML_HEADS),
         zeros(LANES)], axis=-1)
    assert packed.shape[-1] == IN_PAD
    return packed


def _pack_w_uq(w_uq):
    lead = w_uq.shape[:-1]
    w = w_uq.reshape(lead + (MLA_HEADS, MLA_NOPE + MLA_ROPE)).astype(BF16)
    w = jnp.pad(w, [(0, 0)] * len(lead) + [(0, 0), (0, MLA_QK_PAD - MLA_NOPE - MLA_ROPE)])
    return w.reshape(lead + (MLA_HEADS * MLA_QK_PAD,))


def _rope_tables(S):
    half = MLA_ROPE // 2
    inv_freq = ROPE_THETA ** (-jnp.arange(0, MLA_ROPE, 2, dtype=F32) / MLA_ROPE)
    ang = jnp.arange(S, dtype=F32)[:, None] * inv_freq[None, :]
    cos, sin = jnp.cos(ang), jnp.sin(ang)
    z = lambda n: jnp.zeros((S, n), F32)
    cosf = jnp.concatenate([cos, cos, z(LANES - MLA_ROPE)], axis=1)
    sin_lo = jnp.concatenate([-sin, z(LANES - half)], axis=1)
    sin_hi = jnp.concatenate([z(half), sin, z(LANES - MLA_ROPE)], axis=1)
    return cosf, sin_lo, sin_hi


def _layer(x, tables, norm_g, w_in_p, mla_gq, mla_gkv, w_uq_p, w_ukv_b, sg_ln_g, sg_ln_b, sg_ws, sg_bs,
           ml_conv_w, ml_conv_b, ml_wq_b, ml_wk_b, ml_bi, ml_bf, ml_gn_g, w_pa_b, w_pb_b, w_pc_b, w_out_b):
    h = _rmsnorm(x, norm_g, BF16)
    proj = _in_proj(h, w_in_p)
    q = _q_proj(proj, mla_gq, w_uq_p, *tables)
    k, v = _kv_proj(proj, mla_gkv, w_ukv_b, *tables)
    y_a = _attention(q, k, v, proj)
    y_b = _spatial_gating(proj, sg_ln_g, sg_ln_b, sg_ws, sg_bs)
    y_c = _mlstm(proj, ml_conv_w, ml_conv_b, ml_wq_b, ml_wk_b, ml_bi, ml_bf, ml_gn_g)
    merged = _merge(y_a, y_b, y_c, w_pa_b, w_pb_b, w_pc_b, proj)
    return _out_proj(merged, w_out_b, x)


def kernel(x, norm_g, w_in, mla_gq, mla_gkv, mla_wuq, mla_wukv, sg_ln_g, sg_ln_b, sg_ws, sg_bs, ml_conv_w, ml_conv_b, ml_wq, ml_wk, ml_bi, ml_bf, ml_gn_g, w_pa, w_pb, w_pc, w_out, final_g):
    B, S, D = x.shape
    depth = w_in.shape[0]
    tables = _rope_tables(S)
    cast = lambda w: w.astype(BF16)
    outs = []
    for b in range(B):
        xb = x[b]
        for l in range(depth):
            xb = _layer(xb, tables, norm_g[l], _pack_w_in(w_in[l]), mla_gq[l], mla_gkv[l],
                        _pack_w_uq(mla_wuq[l]), cast(mla_wukv[l]),
                        sg_ln_g[l], sg_ln_b[l], sg_ws[l], sg_bs[l], ml_conv_w[l], ml_conv_b[l],
                        cast(ml_wq[l]), cast(ml_wk[l]), ml_bi[l], ml_bf[l], ml_gn_g[l],
                        cast(w_pa[l]), cast(w_pb[l]), cast(w_pc[l]), cast(w_out[l]))
        outs.append(_rmsnorm(xb, final_g, x.dtype))
    return outs[0][None] if B == 1 else jnp.stack(outs, axis=0)
```

```python
import functools

import jax
import jax.numpy as jnp
from jax import lax
from jax.experimental import pallas as pl
from jax.experimental.pallas import tpu as pltpu

F32 = jnp.float32
BF16 = jnp.bfloat16

D_MODEL = 4096
CHUNK = 64
EPS = 1e-6

MLA_HEADS = 16
MLA_NOPE = 128
MLA_ROPE = 64
MLA_V = 128
Q_RANK = 1024
KV_RANK = 512
ROPE_THETA = 10000.0
MLA_WIDTH = MLA_HEADS * MLA_V
MLA_QK_PAD = 256

SG_BLOCK = 128
SG_GROUPS = 4
SG_WIDTH = 1024
SG_GROUP_DIM = SG_WIDTH // SG_GROUPS

ML_HEADS = 4
ML_WIDTH = 1024
ML_HEAD_DIM = 256
CONV_K = 4
ML_CHUNK = 128

LANES = 128
NEG = -1e30

COL_CQ = 0
COL_ZA = 1024
COL_SGU = 3072
COL_SGV = 4096
COL_ZB = 5120
COL_MLX = 6144
COL_MLV = 7168
COL_MLO = 8192
COL_ZC = 9216
COL_GATE = 10240
COL_CKV = 22528
COL_KR = 23040
COL_MLI = 23168
COL_MLF = 23296
IN_PAD = 23552

VMEM_LIMIT = 56 * 1024 * 1024


def _cparams(n_axes, vmem=VMEM_LIMIT):
    return pltpu.CompilerParams(dimension_semantics=("arbitrary",) * n_axes,
                                vmem_limit_bytes=vmem)


def _sigmoid(x):
    return 1.0 / (1.0 + jnp.exp(-x))


def _silu(x):
    return x * _sigmoid(x)


def _rmsnorm_body(x_ref, g_ref, o_ref):
    x = x_ref[...]
    y = x * lax.rsqrt(jnp.mean(x * x, axis=-1, keepdims=True) + EPS)
    o_ref[...] = (y * g_ref[...]).astype(o_ref.dtype)


def _rmsnorm(x, g, out_dtype):
    S, D = x.shape
    tm = min(256, S)
    return pl.pallas_call(
        _rmsnorm_body,
        grid=(S // tm,),
        in_specs=[pl.BlockSpec((tm, D), lambda i: (i, 0)),
                  pl.BlockSpec((1, D), lambda i: (0, 0))],
        out_specs=pl.BlockSpec((tm, D), lambda i: (i, 0)),
        out_shape=jax.ShapeDtypeStruct((S, D), out_dtype),
        compiler_params=_cparams(1),
        name="rmsnorm",
    )(x, g.reshape(1, D))


def _matmul_body(a_ref, b_ref, o_ref):
    o_ref[...] = jnp.dot(a_ref[...], b_ref[...], preferred_element_type=F32).astype(o_ref.dtype)


def _in_proj(h, w):
    S, K = h.shape
    N = w.shape[1]
    tm = min(1024, S)
    tn = 1024
    return pl.pallas_call(
        _matmul_body,
        grid=(N // tn, S // tm),
        in_specs=[pl.BlockSpec((tm, K), lambda j, i: (i, 0)),
                  pl.BlockSpec((K, tn), lambda j, i: (0, j))],
        out_specs=pl.BlockSpec((tm, tn), lambda j, i: (i, j)),
        out_shape=jax.ShapeDtypeStruct((S, N), F32),
        compiler_params=_cparams(2),
        name="in_proj",
    )(h, w)


def _rope(seg, cosf, sin_lo, sin_hi):
    return (seg * cosf
            + pltpu.roll(seg, LANES - MLA_ROPE // 2, 1) * sin_lo
            + pltpu.roll(seg, MLA_ROPE // 2, 1) * sin_hi)


def _qproj_body(cq_ref, g_ref, w_ref, cos_ref, slo_ref, shi_ref, o_ref, a_scr):
    @pl.when(pl.program_id(1) == 0)
    def _():
        c = cq_ref[...]
        y = c * lax.rsqrt(jnp.mean(c * c, axis=-1, keepdims=True) + EPS)
        a_scr[...] = (y * g_ref[...]).astype(BF16)

    res = jnp.dot(a_scr[...], w_ref[...], preferred_element_type=F32) * ATTN_EXP2_SCALE
    cosf, slo, shi = cos_ref[...], slo_ref[...], shi_ref[...]
    for hh in range(o_ref.shape[0]):
        base = hh * MLA_QK_PAD
        o_ref[hh, :, 0:MLA_NOPE] = res[:, base:base + MLA_NOPE].astype(BF16)
        o_ref[hh, :, MLA_NOPE:MLA_QK_PAD] = _rope(
            res[:, base + MLA_NOPE:base + MLA_QK_PAD], cosf, slo, shi).astype(BF16)


def _q_proj(proj, g, w, cosf, slo, shi):
    S = proj.shape[0]
    tm = min(1024, S)
    hpb = 4
    tn = hpb * MLA_QK_PAD
    row = lambda i, j: (i, 0)
    return pl.pallas_call(
        _qproj_body,
        grid=(S // tm, MLA_HEADS // hpb),
        in_specs=[pl.BlockSpec((tm, Q_RANK), lambda i, j: (i, COL_CQ // Q_RANK)),
                  pl.BlockSpec((1, Q_RANK), lambda i, j: (0, 0)),
                  pl.BlockSpec((Q_RANK, tn), lambda i, j: (0, j)),
                  pl.BlockSpec((tm, LANES), row),
                  pl.BlockSpec((tm, LANES), row),
                  pl.BlockSpec((tm, LANES), row)],
        out_specs=pl.BlockSpec((hpb, tm, MLA_QK_PAD), lambda i, j: (j, i, 0)),
        out_shape=jax.ShapeDtypeStruct((MLA_HEADS, S, MLA_QK_PAD), BF16),
        scratch_shapes=[pltpu.VMEM((tm, Q_RANK), BF16)],
        compiler_params=_cparams(2),
        name="mla_q_proj",
    )(proj, g.reshape(1, Q_RANK), w, cosf, slo, shi)


def _kvproj_body(ckv_ref, kr_ref, g_ref, w_ref, cos_ref, slo_ref, shi_ref, k_ref, v_ref, a_scr):
    @pl.when(pl.program_id(1) == 0)
    def _():
        c = ckv_ref[...]
        y = c * lax.rsqrt(jnp.mean(c * c, axis=-1, keepdims=True) + EPS)
        a_scr[...] = (y * g_ref[...]).astype(BF16)

    res = jnp.dot(a_scr[...], w_ref[...], preferred_element_type=F32)
    k_rot = _rope(kr_ref[...], cos_ref[...], slo_ref[...], shi_ref[...]).astype(BF16)
    width = MLA_NOPE + MLA_V
    for hh in range(k_ref.shape[0]):
        base = hh * width
        k_ref[hh, :, 0:MLA_NOPE] = res[:, base:base + MLA_NOPE].astype(BF16)
        k_ref[hh, :, MLA_NOPE:MLA_QK_PAD] = k_rot
        v_ref[hh] = res[:, base + MLA_NOPE:base + width].astype(BF16)


def _kv_proj(proj, g, w, cosf, slo, shi):
    S = proj.shape[0]
    tm = min(1024, S)
    hpb = 4
    tn = hpb * (MLA_NOPE + MLA_V)
    row = lambda i, j: (i, 0)
    return pl.pallas_call(
        _kvproj_body,
        grid=(S // tm, MLA_HEADS // hpb),
        in_specs=[pl.BlockSpec((tm, KV_RANK), lambda i, j: (i, COL_CKV // KV_RANK)),
                  pl.BlockSpec((tm, LANES), lambda i, j: (i, COL_KR // LANES)),
                  pl.BlockSpec((1, KV_RANK), lambda i, j: (0, 0)),
                  pl.BlockSpec((KV_RANK, tn), lambda i, j: (0, j)),
                  pl.BlockSpec((tm, LANES), row),
                  pl.BlockSpec((tm, LANES), row),
                  pl.BlockSpec((tm, LANES), row)],
        out_specs=[pl.BlockSpec((hpb, tm, MLA_QK_PAD), lambda i, j: (j, i, 0)),
                   pl.BlockSpec((hpb, tm, MLA_V), lambda i, j: (j, i, 0))],
        out_shape=[jax.ShapeDtypeStruct((MLA_HEADS, S, MLA_QK_PAD), BF16),
                   jax.ShapeDtypeStruct((MLA_HEADS, S, MLA_V), BF16)],
        scratch_shapes=[pltpu.VMEM((tm, KV_RANK), BF16)],
        compiler_params=_cparams(2),
        name="mla_kv_proj",
    )(proj, proj, g.reshape(1, KV_RANK), w, cosf, slo, shi)


ATTN_HEADS_PER_STEP = 2
ATTN_EXP2_SCALE = float(MLA_NOPE + MLA_ROPE) ** -0.5 * 1.4426950408889634
ATTN_STRIP = 64


def _attn_body(q_ref, k_ref, v_ref, z_ref, o_ref, s_scr, p_scr, m_scr, a_scr, acc_scr, *, blk):
    i = pl.program_id(1)
    A, B = 0, 1
    ones_col = (lax.broadcasted_iota(jnp.int32, (blk, MLA_V), 1) == 0).astype(BF16)
    col_chunk = lax.broadcasted_iota(jnp.int32, (ATTN_STRIP, blk), 1) // CHUNK
    row_in_strip = lax.broadcasted_iota(jnp.int32, (ATTN_STRIP, blk), 0)

    def score(h, kb):
        k = k_ref[h, pl.ds(pl.multiple_of(kb * blk, blk), blk), :]
        s_scr[h] = lax.dot_general(q_ref[h], k, (((1,), (1,)), ((), ())), preferred_element_type=F32)

    def softmax(h, diagonal):
        for r in range(blk // ATTN_STRIP):
            rows = slice(r * ATTN_STRIP, (r + 1) * ATTN_STRIP)
            s = s_scr[h, rows, :]
            if diagonal:
                s = jnp.where(col_chunk <= (row_in_strip + r * ATTN_STRIP) // CHUNK, s, NEG)
                m_new = jnp.max(s, axis=-1, keepdims=True)
                a_scr[h, rows, :] = jnp.ones((ATTN_STRIP, 1), F32)
            else:
                m_old = m_scr[h, rows, :]
                m_new = jnp.maximum(m_old, jnp.max(s, axis=-1, keepdims=True))
                a_scr[h, rows, :] = jnp.exp2(m_old - m_new)
            m_scr[h, rows, :] = m_new
            p_scr[h, rows, :] = jnp.exp2(s - m_new).astype(BF16)

    def apply_values(h, kb):
        v = v_ref[h, pl.ds(pl.multiple_of(kb * blk, blk), blk), :]
        pv = jnp.dot(p_scr[h], jnp.concatenate([v, ones_col], axis=1), preferred_element_type=F32)
        acc_scr[h] = a_scr[h] * acc_scr[h] + pv

    acc_scr[...] = jnp.zeros_like(acc_scr)
    score(A, i)
    softmax(A, True)
    score(B, i)
    softmax(B, True)
    score(A, 0)
    apply_values(A, i)

    def body(j, carry):
        softmax(A, False)
        score(B, j - 1)
        apply_values(B, jnp.where(j == 1, i, j - 2))
        softmax(B, False)
        score(A, j)
        apply_values(A, j - 1)
        return carry

    lax.fori_loop(1, i + 1, body, 0)
    apply_values(B, jnp.maximum(i - 1, 0))
    for h in (A, B):
        cols = slice(h * MLA_V, (h + 1) * MLA_V)
        attn = acc_scr[h, :, 0:MLA_V] / acc_scr[h, :, MLA_V:MLA_V + 1]
        o_ref[:, cols] = (attn * _silu(z_ref[:, cols])).astype(o_ref.dtype)


def _attention(q, k, v, proj):
    H, S, _ = q.shape
    blk = min(512, S)
    hp = ATTN_HEADS_PER_STEP
    assert hp == 2
    return pl.pallas_call(
        functools.partial(_attn_body, blk=blk),
        grid=(H // hp, S // blk),
        in_specs=[pl.BlockSpec((hp, blk, MLA_QK_PAD), lambda h, i: (h, i, 0)),
                  pl.BlockSpec((hp, S, MLA_QK_PAD), lambda h, i: (h, 0, 0)),
                  pl.BlockSpec((hp, S, MLA_V), lambda h, i: (h, 0, 0)),
                  pl.BlockSpec((blk, hp * MLA_V), lambda h, i: (i, COL_ZA // (hp * MLA_V) + h))],
        out_specs=pl.BlockSpec((blk, hp * MLA_V), lambda h, i: (i, h)),
        out_shape=jax.ShapeDtypeStruct((S, MLA_WIDTH), BF16),
        scratch_shapes=[pltpu.VMEM((hp, blk, blk), F32), pltpu.VMEM((hp, blk, blk), BF16),
                        pltpu.VMEM((hp, blk, 1), F32), pltpu.VMEM((hp, blk, 1), F32),
                        pltpu.VMEM((hp, blk, 2 * MLA_V), F32)],
        compiler_params=_cparams(2),
        name="mla_attention",
    )(q, k, v, proj)


def _sgu_body(u_ref, v_ref, z_ref, lng_ref, lnb_ref, ws_ref, bst_ref, o_ref):
    v = v_ref[...]
    mu = jnp.mean(v, axis=-1, keepdims=True)
    vc = v - mu
    var = jnp.mean(vc * vc, axis=-1, keepdims=True)
    vn = (vc * lax.rsqrt(var + EPS) * lng_ref[...] + lnb_ref[...]).astype(BF16)
    out_chunk = lax.broadcasted_iota(jnp.int32, (SG_BLOCK, SG_BLOCK), 0) // CHUNK
    in_chunk = lax.broadcasted_iota(jnp.int32, (SG_BLOCK, SG_BLOCK), 1) // CHUNK
    causal = in_chunk <= out_chunk
    bst = bst_ref[...]
    for g in range(SG_GROUPS):
        w = jnp.where(causal, ws_ref[g], 0.0).astype(BF16)
        cols = slice(g * SG_GROUP_DIM, (g + 1) * SG_GROUP_DIM)
        bias = bst[:, g:g + 1]
        for b in range(u_ref.shape[0] // SG_BLOCK):
            rows = slice(b * SG_BLOCK, (b + 1) * SG_BLOCK)
            s = jnp.dot(w, vn[rows, cols], preferred_element_type=F32) + bias
            o_ref[rows, cols] = (u_ref[rows, cols] * s * _silu(z_ref[rows, cols])).astype(o_ref.dtype)


def _spatial_gating(proj, ln_g, ln_b, w_s, b_s):
    S = proj.shape[0]
    tm = min(512, S)
    blk = lambda c: pl.BlockSpec((tm, SG_WIDTH), lambda i: (i, c // SG_WIDTH))
    full = lambda shape: pl.BlockSpec(shape, lambda i: (0,) * len(shape))
    return pl.pallas_call(
        _sgu_body,
        grid=(S // tm,),
        in_specs=[blk(COL_SGU), blk(COL_SGV), blk(COL_ZB),
                  full((1, SG_WIDTH)), full((1, SG_WIDTH)),
                  full((SG_GROUPS, SG_BLOCK, SG_BLOCK)), full((SG_BLOCK, SG_GROUPS))],
        out_specs=pl.BlockSpec((tm, SG_WIDTH), lambda i: (i, 0)),
        out_shape=jax.ShapeDtypeStruct((S, SG_WIDTH), BF16),
        compiler_params=_cparams(1),
        name="spatial_gating",
    )(proj, proj, proj, ln_g.reshape(1, SG_WIDTH), ln_b.reshape(1, SG_WIDTH), w_s, jnp.transpose(b_s))


def _mlstm_body(x_ref, v_ref, og_ref, z_ref, i_ref, f_ref, cw_ref, cb_ref, wq_ref, wk_ref,
                bi_ref, bf_ref, gn_ref, o_ref, xs_scr, ct_scr, n_scr, m_scr, h_scr, *, T, L):
    halo = 8

    @pl.when(pl.program_id(0) == 0)
    def _():
        xs_scr[0:halo, :] = jnp.zeros((halo, ML_WIDTH), F32)
        ct_scr[...] = jnp.zeros_like(ct_scr)
        n_scr[...] = jnp.zeros_like(n_scr)
        m_scr[...] = jnp.zeros_like(m_scr)

    xs_scr[halo:halo + T, :] = x_ref[...]
    cw = cw_ref[...]
    xc = cb_ref[...]
    for kk in range(CONV_K):
        off = halo - (CONV_K - 1) + kk
        xc = xc + cw[kk:kk + 1, :] * xs_scr[off:off + T, :]
    xs_scr[0:halo, :] = xs_scr[T:T + halo, :]
    xcb = _silu(xc).astype(BF16)

    ig = i_ref[...] + bi_ref[...]
    fg = f_ref[...] + bf_ref[...]
    lf = jnp.minimum(fg, 0.0) - jnp.log1p(jnp.exp(-jnp.abs(fg)))
    rr = lax.broadcasted_iota(jnp.int32, (L, L), 0)
    cc = lax.broadcasted_iota(jnp.int32, (L, L), 1)
    tril = cc <= rr
    tri_f = tril.astype(F32)
    nc = T // L
    g_l, G_l, a_l, e_l, r_l = [], [], [], [], []
    for c in range(nc):
        rows = slice(c * L, (c + 1) * L)
        g_c = jnp.dot(tri_f, lf[rows], precision=lax.Precision.HIGHEST, preferred_element_type=F32)
        G_c = g_c[L - 1:L, :]
        w_end = G_c - g_c + ig[rows]
        a_c = jnp.max(w_end, axis=0, keepdims=True)
        g_l.append(g_c)
        G_l.append(G_c)
        a_l.append(a_c)
        e_l.append(jnp.exp(w_end - a_c))
        r_l.append(ig[rows] - g_c)
    r_t = jnp.transpose(jnp.concatenate(r_l, axis=0))

    q_l, k_l = [], []
    for h in range(ML_HEADS):
        cols = slice(h * ML_HEAD_DIM, (h + 1) * ML_HEAD_DIM)
        q_l.append(jnp.dot(xcb[:, cols], wq_ref[h], preferred_element_type=F32))
        k_l.append(jnp.dot(xcb[:, cols], wk_ref[h], preferred_element_type=F32) * (ML_HEAD_DIM ** -0.5))

    m_vec = m_scr[0:1, :]
    for c in range(nc):
        rows = slice(c * L, (c + 1) * L)
        g_c, G_c, a_c, e_c = g_l[c], G_l[c], a_l[c], e_l[c]
        m_new = jnp.maximum(G_c + m_vec, a_c)
        sp = jnp.exp(G_c + m_vec - m_new)
        sl = jnp.exp(a_c - m_new)
        for h in range(ML_HEADS):
            cols = slice(h * ML_HEAD_DIM, (h + 1) * ML_HEAD_DIM)
            q_c = q_l[h][rows]
            k_c = k_l[h][rows]
            v_c = v_ref[rows, cols]
            q_cb = q_c.astype(BF16)
            g_col = g_c[:, h:h + 1]
            d = jnp.where(tril, g_col + r_t[h:h + 1, rows], NEG)
            inter_log = g_col + m_vec[:, h:h + 1]
            m_i = jnp.maximum(jnp.max(d, axis=1, keepdims=True), inter_log)
            p = jnp.exp(d - m_i)
            qk = lax.dot_general(q_cb, k_c.astype(BF16), (((1,), (1,)), ((), ())),
                                 preferred_element_type=F32)
            sm = qk * p
            inter_scale = jnp.exp(inter_log - m_i)
            ct = ct_scr[h]
            n_prev = n_scr[h:h + 1, :]
            num = (jnp.dot(sm.astype(BF16), v_c.astype(BF16), preferred_element_type=F32)
                   + inter_scale * jnp.dot(q_cb, ct.astype(BF16), preferred_element_type=F32))
            den = (jnp.sum(sm, axis=1, keepdims=True)
                   + inter_scale * jnp.sum(q_c * n_prev, axis=1, keepdims=True))
            h_scr[rows, cols] = num / jnp.maximum(jnp.abs(den), jnp.exp(-m_i))

            e_col = e_c[:, h:h + 1]
            ev = (e_col * v_c).astype(BF16)
            c_loc_t = jnp.dot(jnp.transpose(k_c).astype(BF16), ev, preferred_element_type=F32)
            sp_h = sp[:, h:h + 1]
            sl_h = sl[:, h:h + 1]
            ct_scr[h] = sp_h * ct + sl_h * c_loc_t
            n_scr[h:h + 1, :] = sp_h * n_prev + sl_h * jnp.sum(e_col * k_c, axis=0, keepdims=True)
        m_vec = m_new
    m_scr[0:1, :] = m_vec

    for h in range(ML_HEADS):
        cols = slice(h * ML_HEAD_DIM, (h + 1) * ML_HEAD_DIM)
        hs = h_scr[:, cols] * _sigmoid(og_ref[:, cols])
        mu = jnp.mean(hs, axis=-1, keepdims=True)
        hc = hs - mu
        var = jnp.mean(hc * hc, axis=-1, keepdims=True)
        y = hc * lax.rsqrt(var + EPS) * gn_ref[:, cols]
        o_ref[:, cols] = (y * _silu(z_ref[:, cols])).astype(o_ref.dtype)


def _mlstm(proj, conv_w, conv_b, w_q, w_k, b_i, b_f, gn_g):
    S = proj.shape[0]
    T = min(512, S)
    L = ML_CHUNK
    wide = lambda c: pl.BlockSpec((T, ML_WIDTH), lambda i: (i, c // ML_WIDTH))
    gate = lambda c: pl.BlockSpec((T, LANES), lambda i: (i, c // LANES))
    full = lambda shape: pl.BlockSpec(shape, lambda i: (0,) * len(shape))
    pad_gate_bias = lambda b: jnp.pad(b, (0, LANES - ML_HEADS)).reshape(1, LANES)
    return pl.pallas_call(
        functools.partial(_mlstm_body, T=T, L=L),
        grid=(S // T,),
        in_specs=[wide(COL_MLX), wide(COL_MLV), wide(COL_MLO), wide(COL_ZC),
                  gate(COL_MLI), gate(COL_MLF),
                  full((CONV_K, ML_WIDTH)), full((1, ML_WIDTH)),
                  full((ML_HEADS, ML_HEAD_DIM, ML_HEAD_DIM)), full((ML_HEADS, ML_HEAD_DIM, ML_HEAD_DIM)),
                  full((1, LANES)), full((1, LANES)), full((1, ML_WIDTH))],
        out_specs=pl.BlockSpec((T, ML_WIDTH), lambda i: (i, 0)),
        out_shape=jax.ShapeDtypeStruct((S, ML_WIDTH), BF16),
        scratch_shapes=[pltpu.VMEM((T + 8, ML_WIDTH), F32),
                        pltpu.VMEM((ML_HEADS, ML_HEAD_DIM, ML_HEAD_DIM), F32),
                        pltpu.VMEM((8, ML_HEAD_DIM), F32),
                        pltpu.VMEM((8, LANES), F32),
                        pltpu.VMEM((T, ML_WIDTH), F32)],
        compiler_params=_cparams(1),
        name="mlstm",
    )(proj, proj, proj, proj, proj, proj, conv_w, conv_b.reshape(1, ML_WIDTH), w_q, w_k,
      pad_gate_bias(b_i), pad_gate_bias(b_f), gn_g.reshape(1, ML_WIDTH))


def _merge_body(ya_ref, yb_ref, yc_ref, wa_ref, wb_ref, wc_ref, g0_ref, g1_ref, g2_ref, o_ref, wa_s, wb_s, wc_s):
    @pl.when(pl.program_id(1) == 0)
    def _():
        wa_s[...] = wa_ref[...].astype(BF16)
        wb_s[...] = wb_ref[...].astype(BF16)
        wc_s[...] = wc_ref[...].astype(BF16)

    pa = jnp.dot(ya_ref[...], wa_s[...], preferred_element_type=F32)
    pb = jnp.dot(yb_ref[...], wb_s[...], preferred_element_type=F32)
    pc = jnp.dot(yc_ref[...], wc_s[...], preferred_element_type=F32)
    merged = _sigmoid(g0_ref[...]) * pa + _sigmoid(g1_ref[...]) * pb + _sigmoid(g2_ref[...]) * pc
    o_ref[...] = merged.astype(o_ref.dtype)


def _merge(y_a, y_b, y_c, w_pa, w_pb, w_pc, layer, proj):
    S = y_a.shape[0]
    D = w_pa.shape[-1]
    tm = min(512, S)
    tn = 512
    act = lambda width: pl.BlockSpec((tm, width), lambda j, i: (i, 0))
    wgt = lambda width: pl.BlockSpec((None, width, tn), lambda j, i: (layer, 0, j))
    gate = lambda b: pl.BlockSpec((tm, tn), lambda j, i: (i, (COL_GATE + b * D) // tn + j))
    return pl.pallas_call(
        _merge_body,
        grid=(D // tn, S // tm),
        in_specs=[act(MLA_WIDTH), act(SG_WIDTH), act(ML_WIDTH),
                  wgt(MLA_WIDTH), wgt(SG_WIDTH), wgt(ML_WIDTH),
                  gate(0), gate(1), gate(2)],
        out_specs=pl.BlockSpec((tm, tn), lambda j, i: (i, j)),
        out_shape=jax.ShapeDtypeStruct((S, D), BF16),
        scratch_shapes=[pltpu.VMEM((MLA_WIDTH, tn), BF16), pltpu.VMEM((SG_WIDTH, tn), BF16),
                        pltpu.VMEM((ML_WIDTH, tn), BF16)],
        compiler_params=_cparams(2),
        name="gated_merge",
    )(y_a, y_b, y_c, w_pa, w_pb, w_pc, proj, proj, proj)


def _outproj_body(m_ref, w_ref, x_ref, o_ref, w_s):
    @pl.when(pl.program_id(1) == 0)
    def _():
        w_s[...] = w_ref[...].astype(BF16)

    o_ref[...] = x_ref[...] + jnp.dot(m_ref[...], w_s[...], preferred_element_type=F32)


def _out_proj(merged, w_out, layer, x):
    S, D = x.shape
    tm = min(1024, S)
    tn = 512
    return pl.pallas_call(
        _outproj_body,
        grid=(D // tn, S // tm),
        in_specs=[pl.BlockSpec((tm, D), lambda j, i: (i, 0)),
                  pl.BlockSpec((None, D, tn), lambda j, i: (layer, 0, j)),
                  pl.BlockSpec((tm, tn), lambda j, i: (i, j))],
        out_specs=pl.BlockSpec((tm, tn), lambda j, i: (i, j)),
        out_shape=jax.ShapeDtypeStruct((S, D), F32),
        scratch_shapes=[pltpu.VMEM((D, tn), BF16)],
        compiler_params=_cparams(2),
        name="out_proj",
    )(merged, w_out, x)


_IN_SEGMENTS = (
    (0, Q_RANK, COL_CQ), (1024, KV_RANK, COL_CKV), (1536, MLA_ROPE, COL_KR), (1600, MLA_WIDTH, COL_ZA),
    (3648, SG_WIDTH, COL_SGU), (4672, SG_WIDTH, COL_SGV), (5696, SG_WIDTH, COL_ZB),
    (6720, ML_WIDTH, COL_MLX), (7744, ML_WIDTH, COL_MLV), (8768, ML_WIDTH, COL_MLO), (9792, ML_WIDTH, COL_ZC),
    (10816, ML_HEADS, COL_MLI), (10820, ML_HEADS, COL_MLF), (10824, 3 * D_MODEL, COL_GATE),
)
IN_WIDTH = 10824 + 3 * D_MODEL


def _pack_w_in_body(w_ref, o_ref):
    o_ref[:, COL_KR:IN_PAD] = jnp.zeros((o_ref.shape[0], IN_PAD - COL_KR), BF16)
    for src, width, dst in _IN_SEGMENTS:
        o_ref[:, dst:dst + width] = w_ref[:, src:src + width].astype(BF16)


def _pack_w_in(w_in, layer):
    D = w_in.shape[1]
    tr = 64
    return pl.pallas_call(
        _pack_w_in_body,
        grid=(D // tr,),
        in_specs=[pl.BlockSpec((None, tr, IN_WIDTH), lambda i: (layer, i, 0))],
        out_specs=pl.BlockSpec((tr, IN_PAD), lambda i: (i, 0)),
        out_shape=jax.ShapeDtypeStruct((D, IN_PAD), BF16),
        compiler_params=_cparams(1),
        name="pack_w_in",
    )(w_in)


def _pack_w_uq(w_uq):
    lead = w_uq.shape[:-1]
    w = w_uq.reshape(lead + (MLA_HEADS, MLA_NOPE + MLA_ROPE)).astype(BF16)
    w = jnp.pad(w, [(0, 0)] * len(lead) + [(0, 0), (0, MLA_QK_PAD - MLA_NOPE - MLA_ROPE)])
    return w.reshape(lead + (MLA_HEADS * MLA_QK_PAD,))


def _rope_tables(S):
    half = MLA_ROPE // 2
    inv_freq = ROPE_THETA ** (-jnp.arange(0, MLA_ROPE, 2, dtype=F32) / MLA_ROPE)
    ang = jnp.arange(S, dtype=F32)[:, None] * inv_freq[None, :]
    cos, sin = jnp.cos(ang), jnp.sin(ang)
    z = lambda n: jnp.zeros((S, n), F32)
    cosf = jnp.concatenate([cos, cos, z(LANES - MLA_ROPE)], axis=1)
    sin_lo = jnp.concatenate([-sin, z(LANES - half)], axis=1)
    sin_hi = jnp.concatenate([z(half), sin, z(LANES - MLA_ROPE)], axis=1)
    return cosf, sin_lo, sin_hi


def _layer(x, tables, layer, norm_g, w_in_p, mla_gq, mla_gkv, w_uq_p, w_ukv_b, sg_ln_g, sg_ln_b, sg_ws, sg_bs,
           ml_conv_w, ml_conv_b, ml_wq_b, ml_wk_b, ml_bi, ml_bf, ml_gn_g, w_pa, w_pb, w_pc, w_out):
    h = _rmsnorm(x, norm_g, BF16)
    proj = _in_proj(h, w_in_p)
    q = _q_proj(proj, mla_gq, w_uq_p, *tables)
    k, v = _kv_proj(proj, mla_gkv, w_ukv_b, *tables)
    y_a = _attention(q, k, v, proj)
    y_b = _spatial_gating(proj, sg_ln_g, sg_ln_b, sg_ws, sg_bs)
    y_c = _mlstm(proj, ml_conv_w, ml_conv_b, ml_wq_b, ml_wk_b, ml_bi, ml_bf, ml_gn_g)
    merged = _merge(y_a, y_b, y_c, w_pa, w_pb, w_pc, layer, proj)
    return _out_proj(merged, w_out, layer, x)


def kernel(x, norm_g, w_in, mla_gq, mla_gkv, mla_wuq, mla_wukv, sg_ln_g, sg_ln_b, sg_ws, sg_bs, ml_conv_w, ml_conv_b, ml_wq, ml_wk, ml_bi, ml_bf, ml_gn_g, w_pa, w_pb, w_pc, w_out, final_g):
    B, S, D = x.shape
    depth = w_in.shape[0]
    tables = _rope_tables(S)
    cast = lambda w: w.astype(BF16)
    outs = []
    for b in range(B):
        xb = x[b]
        for l in range(depth):
            xb = _layer(xb, tables, l, norm_g[l], _pack_w_in(w_in, l), mla_gq[l], mla_gkv[l],
                        _pack_w_uq(mla_wuq[l]), cast(mla_wukv[l]),
                        sg_ln_g[l], sg_ln_b[l], sg_ws[l], sg_bs[l], ml_conv_w[l], ml_conv_b[l],
                        cast(ml_wq[l]), cast(ml_wk[l]), ml_bi[l], ml_bf[l], ml_gn_g[l],
                        w_pa, w_pb, w_pc, w_out)
        outs.append(_rmsnorm(xb, final_g, x.dtype))
    return outs[0][None] if B == 1 else jnp.stack(outs, axis=0)
```

```python
import functools

import jax
import jax.numpy as jnp
from jax import lax
from jax.experimental import pallas as pl
from jax.experimental.pallas import tpu as pltpu

F32 = jnp.float32
BF16 = jnp.bfloat16

D_MODEL = 4096
CHUNK = 64
EPS = 1e-6

MLA_HEADS = 16
MLA_NOPE = 128
MLA_ROPE = 64
MLA_V = 128
Q_RANK = 1024
KV_RANK = 512
ROPE_THETA = 10000.0
MLA_WIDTH = MLA_HEADS * MLA_V
MLA_QK_PAD = 256

SG_BLOCK = 128
SG_GROUPS = 4
SG_WIDTH = 1024
SG_GROUP_DIM = SG_WIDTH // SG_GROUPS

ML_HEADS = 4
ML_WIDTH = 1024
ML_HEAD_DIM = 256
CONV_K = 4
ML_CHUNK = 128

LANES = 128
NEG = -1e30

COL_CQ = 0
COL_ZA = 1024
COL_SGU = 3072
COL_SGV = 4096
COL_ZB = 5120
COL_MLX = 6144
COL_MLV = 7168
COL_MLO = 8192
COL_ZC = 9216
COL_GATE = 10240
COL_CKV = 22528
COL_KR = 23040
COL_MLI = 23168
COL_MLF = 23296
IN_PAD = 23552

VMEM_LIMIT = 56 * 1024 * 1024


def _cparams(n_axes, vmem=VMEM_LIMIT):
    return pltpu.CompilerParams(dimension_semantics=("arbitrary",) * n_axes,
                                vmem_limit_bytes=vmem)


def _sigmoid(x):
    return 1.0 / (1.0 + jnp.exp(-x))


def _silu(x):
    return x * _sigmoid(x)


def _rmsnorm_body(x_ref, g_ref, o_ref):
    x = x_ref[...]
    y = x * lax.rsqrt(jnp.mean(x * x, axis=-1, keepdims=True) + EPS)
    o_ref[...] = (y * g_ref[...]).astype(o_ref.dtype)


def _rmsnorm(x, g, out_dtype):
    S, D = x.shape
    tm = min(256, S)
    return pl.pallas_call(
        _rmsnorm_body,
        grid=(S // tm,),
        in_specs=[pl.BlockSpec((tm, D), lambda i: (i, 0)),
                  pl.BlockSpec((1, D), lambda i: (0, 0))],
        out_specs=pl.BlockSpec((tm, D), lambda i: (i, 0)),
        out_shape=jax.ShapeDtypeStruct((S, D), out_dtype),
        compiler_params=_cparams(1),
        name="rmsnorm",
    )(x, g.reshape(1, D))


def _matmul_body(a_ref, b_ref, o_ref):
    o_ref[...] = jnp.dot(a_ref[...], b_ref[...], preferred_element_type=F32).astype(o_ref.dtype)


def _in_proj(h, w):
    S, K = h.shape
    N = w.shape[1]
    tm = min(1024, S)
    tn = 1024
    return pl.pallas_call(
        _matmul_body,
        grid=(N // tn, S // tm),
        in_specs=[pl.BlockSpec((tm, K), lambda j, i: (i, 0)),
                  pl.BlockSpec((K, tn), lambda j, i: (0, j))],
        out_specs=pl.BlockSpec((tm, tn), lambda j, i: (i, j)),
        out_shape=jax.ShapeDtypeStruct((S, N), F32),
        compiler_params=_cparams(2),
        name="in_proj",
    )(h, w)


def _rope(seg, cosf, sin_lo, sin_hi):
    return (seg * cosf
            + pltpu.roll(seg, LANES - MLA_ROPE // 2, 1) * sin_lo
            + pltpu.roll(seg, MLA_ROPE // 2, 1) * sin_hi)


def _qproj_body(cq_ref, g_ref, w_ref, cos_ref, slo_ref, shi_ref, o_ref, a_scr):
    @pl.when(pl.program_id(1) == 0)
    def _():
        c = cq_ref[...]
        y = c * lax.rsqrt(jnp.mean(c * c, axis=-1, keepdims=True) + EPS)
        a_scr[...] = (y * g_ref[...]).astype(BF16)

    res = jnp.dot(a_scr[...], w_ref[...], preferred_element_type=F32) * ATTN_EXP2_SCALE
    cosf, slo, shi = cos_ref[...], slo_ref[...], shi_ref[...]
    for hh in range(o_ref.shape[0]):
        base = hh * MLA_QK_PAD
        o_ref[hh, :, 0:MLA_NOPE] = res[:, base:base + MLA_NOPE].astype(BF16)
        o_ref[hh, :, MLA_NOPE:MLA_QK_PAD] = _rope(
            res[:, base + MLA_NOPE:base + MLA_QK_PAD], cosf, slo, shi).astype(BF16)


def _q_proj(proj, g, w, cosf, slo, shi):
    S = proj.shape[0]
    tm = min(1024, S)
    hpb = 4
    tn = hpb * MLA_QK_PAD
    row = lambda i, j: (i, 0)
    return pl.pallas_call(
        _qproj_body,
        grid=(S // tm, MLA_HEADS // hpb),
        in_specs=[pl.BlockSpec((tm, Q_RANK), lambda i, j: (i, COL_CQ // Q_RANK)),
                  pl.BlockSpec((1, Q_RANK), lambda i, j: (0, 0)),
                  pl.BlockSpec((Q_RANK, tn), lambda i, j: (0, j)),
                  pl.BlockSpec((tm, LANES), row),
                  pl.BlockSpec((tm, LANES), row),
                  pl.BlockSpec((tm, LANES), row)],
        out_specs=pl.BlockSpec((hpb, tm, MLA_QK_PAD), lambda i, j: (j, i, 0)),
        out_shape=jax.ShapeDtypeStruct((MLA_HEADS, S, MLA_QK_PAD), BF16),
        scratch_shapes=[pltpu.VMEM((tm, Q_RANK), BF16)],
        compiler_params=_cparams(2),
        name="mla_q_proj",
    )(proj, g.reshape(1, Q_RANK), w, cosf, slo, shi)


def _kvproj_body(ckv_ref, kr_ref, g_ref, w_ref, cos_ref, slo_ref, shi_ref, k_ref, v_ref, a_scr):
    @pl.when(pl.program_id(1) == 0)
    def _():
        c = ckv_ref[...]
        y = c * lax.rsqrt(jnp.mean(c * c, axis=-1, keepdims=True) + EPS)
        a_scr[...] = (y * g_ref[...]).astype(BF16)

    res = jnp.dot(a_scr[...], w_ref[...], preferred_element_type=F32)
    k_rot = _rope(kr_ref[...], cos_ref[...], slo_ref[...], shi_ref[...]).astype(BF16)
    width = MLA_NOPE + MLA_V
    for hh in range(k_ref.shape[0]):
        base = hh * width
        k_ref[hh, :, 0:MLA_NOPE] = res[:, base:base + MLA_NOPE].astype(BF16)
        k_ref[hh, :, MLA_NOPE:MLA_QK_PAD] = k_rot
        v_ref[hh] = res[:, base + MLA_NOPE:base + width].astype(BF16)


def _kv_proj(proj, g, w, cosf, slo, shi):
    S = proj.shape[0]
    tm = min(1024, S)
    hpb = 4
    tn = hpb * (MLA_NOPE + MLA_V)
    row = lambda i, j: (i, 0)
    return pl.pallas_call(
        _kvproj_body,
        grid=(S // tm, MLA_HEADS // hpb),
        in_specs=[pl.BlockSpec((tm, KV_RANK), lambda i, j: (i, COL_CKV // KV_RANK)),
                  pl.BlockSpec((tm, LANES), lambda i, j: (i, COL_KR // LANES)),
                  pl.BlockSpec((1, KV_RANK), lambda i, j: (0, 0)),
                  pl.BlockSpec((KV_RANK, tn), lambda i, j: (0, j)),
                  pl.BlockSpec((tm, LANES), row),
                  pl.BlockSpec((tm, LANES), row),
                  pl.BlockSpec((tm, LANES), row)],
        out_specs=[pl.BlockSpec((hpb, tm, MLA_QK_PAD), lambda i, j: (j, i, 0)),
                   pl.BlockSpec((hpb, tm, MLA_V), lambda i, j: (j, i, 0))],
        out_shape=[jax.ShapeDtypeStruct((MLA_HEADS, S, MLA_QK_PAD), BF16),
                   jax.ShapeDtypeStruct((MLA_HEADS, S, MLA_V), BF16)],
        scratch_shapes=[pltpu.VMEM((tm, KV_RANK), BF16)],
        compiler_params=_cparams(2),
        name="mla_kv_proj",
    )(proj, proj, g.reshape(1, KV_RANK), w, cosf, slo, shi)


ATTN_HEADS_PER_STEP = 2
ATTN_EXP2_SCALE = float(MLA_NOPE + MLA_ROPE) ** -0.5 * 1.4426950408889634
ATTN_STRIP = 64


def _attn_body(q_ref, k_ref, v_ref, z_ref, o_ref, s_scr, p_scr, m_scr, a_scr, acc_scr, *, blk):
    i = pl.program_id(1)
    A, B = 0, 1
    ones_col = (lax.broadcasted_iota(jnp.int32, (blk, MLA_V), 1) == 0).astype(BF16)
    col_chunk = lax.broadcasted_iota(jnp.int32, (ATTN_STRIP, blk), 1) // CHUNK
    row_in_strip = lax.broadcasted_iota(jnp.int32, (ATTN_STRIP, blk), 0)

    def score(h, kb):
        k = k_ref[h, pl.ds(pl.multiple_of(kb * blk, blk), blk), :]
        s_scr[h] = lax.dot_general(q_ref[h], k, (((1,), (1,)), ((), ())), preferred_element_type=F32)

    def softmax(h, diagonal):
        for r in range(blk // ATTN_STRIP):
            rows = slice(r * ATTN_STRIP, (r + 1) * ATTN_STRIP)
            s = s_scr[h, rows, :]
            if diagonal:
                s = jnp.where(col_chunk <= (row_in_strip + r * ATTN_STRIP) // CHUNK, s, NEG)
                m_new = jnp.broadcast_to(jnp.max(s, axis=-1, keepdims=True), (ATTN_STRIP, LANES))
                a_scr[h, rows, :] = jnp.ones((ATTN_STRIP, LANES), F32)
            else:
                m_old = m_scr[h, rows, :]
                m_new = jnp.maximum(m_old, jnp.max(s, axis=-1, keepdims=True))
                a_scr[h, rows, :] = jnp.exp2(m_old - m_new)
            m_scr[h, rows, :] = m_new
            p_scr[h, rows, :] = jnp.exp2(s - jnp.concatenate([m_new] * (blk // LANES), axis=1)).astype(BF16)

    def apply_values(h, kb):
        v = v_ref[h, pl.ds(pl.multiple_of(kb * blk, blk), blk), :]
        pv = jnp.dot(p_scr[h], jnp.concatenate([v, ones_col], axis=1), preferred_element_type=F32)
        a = a_scr[h]
        acc_scr[h] = jnp.concatenate([a, a], axis=1) * acc_scr[h] + pv

    acc_scr[...] = jnp.zeros_like(acc_scr)
    score(A, i)
    softmax(A, True)
    score(B, i)
    softmax(B, True)
    score(A, 0)
    apply_values(A, i)

    def step(j):
        softmax(A, False)
        score(B, j - 1)
        apply_values(B, jnp.where(j == 1, i, j - 2))
        softmax(B, False)
        score(A, j)
        apply_values(A, j - 1)

    def two_steps(t, carry):
        step(2 * t + 1)
        step(2 * t + 2)
        return carry

    lax.fori_loop(0, lax.shift_right_logical(i, 1), two_steps, 0)

    @pl.when(lax.bitwise_and(i, 1) == 1)
    def _():
        step(i)

    apply_values(B, jnp.maximum(i - 1, 0))
    for h in (A, B):
        cols = slice(h * MLA_V, (h + 1) * MLA_V)
        attn = acc_scr[h, :, 0:MLA_V] / acc_scr[h, :, MLA_V:MLA_V + 1]
        o_ref[:, cols] = (attn * _silu(z_ref[:, cols])).astype(o_ref.dtype)


def _attention(q, k, v, proj):
    H, S, _ = q.shape
    blk = min(512, S)
    hp = ATTN_HEADS_PER_STEP
    assert hp == 2
    return pl.pallas_call(
        functools.partial(_attn_body, blk=blk),
        grid=(H // hp, S // blk),
        in_specs=[pl.BlockSpec((hp, blk, MLA_QK_PAD), lambda h, i: (h, i, 0)),
                  pl.BlockSpec((hp, S, MLA_QK_PAD), lambda h, i: (h, 0, 0)),
                  pl.BlockSpec((hp, S, MLA_V), lambda h, i: (h, 0, 0)),
                  pl.BlockSpec((blk, hp * MLA_V), lambda h, i: (i, COL_ZA // (hp * MLA_V) + h))],
        out_specs=pl.BlockSpec((blk, hp * MLA_V), lambda h, i: (i, h)),
        out_shape=jax.ShapeDtypeStruct((S, MLA_WIDTH), BF16),
        scratch_shapes=[pltpu.VMEM((hp, blk, blk), F32), pltpu.VMEM((hp, blk, blk), BF16),
                        pltpu.VMEM((hp, blk, LANES), F32), pltpu.VMEM((hp, blk, LANES), F32),
                        pltpu.VMEM((hp, blk, 2 * MLA_V), F32)],
        compiler_params=_cparams(2),
        name="mla_attention",
    )(q, k, v, proj)


def _sgu_body(u_ref, v_ref, z_ref, lng_ref, lnb_ref, ws_ref, bst_ref, o_ref):
    v = v_ref[...]
    mu = jnp.mean(v, axis=-1, keepdims=True)
    vc = v - mu
    var = jnp.mean(vc * vc, axis=-1, keepdims=True)
    vn = (vc * lax.rsqrt(var + EPS) * lng_ref[...] + lnb_ref[...]).astype(BF16)
    out_chunk = lax.broadcasted_iota(jnp.int32, (SG_BLOCK, SG_BLOCK), 0) // CHUNK
    in_chunk = lax.broadcasted_iota(jnp.int32, (SG_BLOCK, SG_BLOCK), 1) // CHUNK
    causal = in_chunk <= out_chunk
    bst = bst_ref[...]
    for g in range(SG_GROUPS):
        w = jnp.where(causal, ws_ref[g], 0.0).astype(BF16)
        cols = slice(g * SG_GROUP_DIM, (g + 1) * SG_GROUP_DIM)
        bias = bst[:, g:g + 1]
        for b in range(u_ref.shape[0] // SG_BLOCK):
            rows = slice(b * SG_BLOCK, (b + 1) * SG_BLOCK)
            s = jnp.dot(w, vn[rows, cols], preferred_element_type=F32) + bias
            o_ref[rows, cols] = (u_ref[rows, cols] * s * _silu(z_ref[rows, cols])).astype(o_ref.dtype)


def _spatial_gating(proj, ln_g, ln_b, w_s, b_s):
    S = proj.shape[0]
    tm = min(512, S)
    blk = lambda c: pl.BlockSpec((tm, SG_WIDTH), lambda i: (i, c // SG_WIDTH))
    full = lambda shape: pl.BlockSpec(shape, lambda i: (0,) * len(shape))
    return pl.pallas_call(
        _sgu_body,
        grid=(S // tm,),
        in_specs=[blk(COL_SGU), blk(COL_SGV), blk(COL_ZB),
                  full((1, SG_WIDTH)), full((1, SG_WIDTH)),
                  full((SG_GROUPS, SG_BLOCK, SG_BLOCK)), full((SG_BLOCK, SG_GROUPS))],
        out_specs=pl.BlockSpec((tm, SG_WIDTH), lambda i: (i, 0)),
        out_shape=jax.ShapeDtypeStruct((S, SG_WIDTH), BF16),
        compiler_params=_cparams(1),
        name="spatial_gating",
    )(proj, proj, proj, ln_g.reshape(1, SG_WIDTH), ln_b.reshape(1, SG_WIDTH), w_s, jnp.transpose(b_s))


def _mlstm_body(x_ref, v_ref, og_ref, z_ref, i_ref, f_ref, cw_ref, cb_ref, wq_ref, wk_ref,
                bi_ref, bf_ref, gn_ref, o_ref, xs_scr, ct_scr, n_scr, m_scr, h_scr, *, T, L):
    halo = 8

    @pl.when(pl.program_id(0) == 0)
    def _():
        xs_scr[0:halo, :] = jnp.zeros((halo, ML_WIDTH), F32)
        ct_scr[...] = jnp.zeros_like(ct_scr)
        n_scr[...] = jnp.zeros_like(n_scr)
        m_scr[...] = jnp.zeros_like(m_scr)

    xs_scr[halo:halo + T, :] = x_ref[...]
    cw = cw_ref[...]
    xc = cb_ref[...]
    for kk in range(CONV_K):
        off = halo - (CONV_K - 1) + kk
        xc = xc + cw[kk:kk + 1, :] * xs_scr[off:off + T, :]
    xs_scr[0:halo, :] = xs_scr[T:T + halo, :]
    xcb = _silu(xc).astype(BF16)

    ig = i_ref[...] + bi_ref[...]
    fg = f_ref[...] + bf_ref[...]
    lf = jnp.minimum(fg, 0.0) - jnp.log1p(jnp.exp(-jnp.abs(fg)))
    rr = lax.broadcasted_iota(jnp.int32, (L, L), 0)
    cc = lax.broadcasted_iota(jnp.int32, (L, L), 1)
    tril = cc <= rr
    tri_f = tril.astype(F32)
    nc = T // L
    g_l, G_l, a_l, e_l, r_l = [], [], [], [], []
    for c in range(nc):
        rows = slice(c * L, (c + 1) * L)
        g_c = jnp.dot(tri_f, lf[rows], precision=lax.Precision.HIGHEST, preferred_element_type=F32)
        G_c = g_c[L - 1:L, :]
        w_end = G_c - g_c + ig[rows]
        a_c = jnp.max(w_end, axis=0, keepdims=True)
        g_l.append(g_c)
        G_l.append(G_c)
        a_l.append(a_c)
        e_l.append(jnp.exp(w_end - a_c))
        r_l.append(ig[rows] - g_c)
    r_t = jnp.transpose(jnp.concatenate(r_l, axis=0))

    q_l, k_l = [], []
    for h in range(ML_HEADS):
        cols = slice(h * ML_HEAD_DIM, (h + 1) * ML_HEAD_DIM)
        q_l.append(jnp.dot(xcb[:, cols], wq_ref[h], preferred_element_type=F32))
        k_l.append(jnp.dot(xcb[:, cols], wk_ref[h], preferred_element_type=F32) * (ML_HEAD_DIM ** -0.5))

    m_vec = m_scr[0:1, :]
    for c in range(nc):
        rows = slice(c * L, (c + 1) * L)
        g_c, G_c, a_c, e_c = g_l[c], G_l[c], a_l[c], e_l[c]
        m_new = jnp.maximum(G_c + m_vec, a_c)
        sp = jnp.exp(G_c + m_vec - m_new)
        sl = jnp.exp(a_c - m_new)
        for h in range(ML_HEADS):
            cols = slice(h * ML_HEAD_DIM, (h + 1) * ML_HEAD_DIM)
            q_c = q_l[h][rows]
            k_c = k_l[h][rows]
            v_c = v_ref[rows, cols]
            q_cb = q_c.astype(BF16)
            g_col = g_c[:, h:h + 1]
            d = jnp.where(tril, g_col + r_t[h:h + 1, rows], NEG)
            inter_log = g_col + m_vec[:, h:h + 1]
            m_i = jnp.maximum(jnp.max(d, axis=1, keepdims=True), inter_log)
            p = jnp.exp(d - m_i)
            qk = lax.dot_general(q_cb, k_c.astype(BF16), (((1,), (1,)), ((), ())),
                                 preferred_element_type=F32)
            sm = qk * p
            inter_scale = jnp.exp(inter_log - m_i)
            ct = ct_scr[h]
            n_prev = n_scr[h:h + 1, :]
            num = (jnp.dot(sm.astype(BF16), v_c.astype(BF16), preferred_element_type=F32)
                   + inter_scale * jnp.dot(q_cb, ct.astype(BF16), preferred_element_type=F32))
            den = (jnp.sum(sm, axis=1, keepdims=True)
                   + inter_scale * jnp.sum(q_c * n_prev, axis=1, keepdims=True))
            h_scr[rows, cols] = num / jnp.maximum(jnp.abs(den), jnp.exp(-m_i))

            e_col = e_c[:, h:h + 1]
            ev = (e_col * v_c).astype(BF16)
            c_loc_t = jnp.dot(jnp.transpose(k_c).astype(BF16), ev, preferred_element_type=F32)
            sp_h = sp[:, h:h + 1]
            sl_h = sl[:, h:h + 1]
            ct_scr[h] = sp_h * ct + sl_h * c_loc_t
            n_scr[h:h + 1, :] = sp_h * n_prev + sl_h * jnp.sum(e_col * k_c, axis=0, keepdims=True)
        m_vec = m_new
    m_scr[0:1, :] = m_vec

    for h in range(ML_HEADS):
        cols = slice(h * ML_HEAD_DIM, (h + 1) * ML_HEAD_DIM)
        hs = h_scr[:, cols] * _sigmoid(og_ref[:, cols])
        mu = jnp.mean(hs, axis=-1, keepdims=True)
        hc = hs - mu
        var = jnp.mean(hc * hc, axis=-1, keepdims=True)
        y = hc * lax.rsqrt(var + EPS) * gn_ref[:, cols]
        o_ref[:, cols] = (y * _silu(z_ref[:, cols])).astype(o_ref.dtype)


def _mlstm(proj, conv_w, conv_b, w_q, w_k, b_i, b_f, gn_g):
    S = proj.shape[0]
    T = min(512, S)
    L = ML_CHUNK
    wide = lambda c: pl.BlockSpec((T, ML_WIDTH), lambda i: (i, c // ML_WIDTH))
    gate = lambda c: pl.BlockSpec((T, LANES), lambda i: (i, c // LANES))
    full = lambda shape: pl.BlockSpec(shape, lambda i: (0,) * len(shape))
    pad_gate_bias = lambda b: jnp.pad(b, (0, LANES - ML_HEADS)).reshape(1, LANES)
    return pl.pallas_call(
        functools.partial(_mlstm_body, T=T, L=L),
        grid=(S // T,),
        in_specs=[wide(COL_MLX), wide(COL_MLV), wide(COL_MLO), wide(COL_ZC),
                  gate(COL_MLI), gate(COL_MLF),
                  full((CONV_K, ML_WIDTH)), full((1, ML_WIDTH)),
                  full((ML_HEADS, ML_HEAD_DIM, ML_HEAD_DIM)), full((ML_HEADS, ML_HEAD_DIM, ML_HEAD_DIM)),
                  full((1, LANES)), full((1, LANES)), full((1, ML_WIDTH))],
        out_specs=pl.BlockSpec((T, ML_WIDTH), lambda i: (i, 0)),
        out_shape=jax.ShapeDtypeStruct((S, ML_WIDTH), BF16),
        scratch_shapes=[pltpu.VMEM((T + 8, ML_WIDTH), F32),
                        pltpu.VMEM((ML_HEADS, ML_HEAD_DIM, ML_HEAD_DIM), F32),
                        pltpu.VMEM((8, ML_HEAD_DIM), F32),
                        pltpu.VMEM((8, LANES), F32),
                        pltpu.VMEM((T, ML_WIDTH), F32)],
        compiler_params=_cparams(1),
        name="mlstm",
    )(proj, proj, proj, proj, proj, proj, conv_w, conv_b.reshape(1, ML_WIDTH), w_q, w_k,
      pad_gate_bias(b_i), pad_gate_bias(b_f), gn_g.reshape(1, ML_WIDTH))


def _merge_body(ya_ref, yb_ref, yc_ref, wa_ref, wb_ref, wc_ref, g0_ref, g1_ref, g2_ref, o_ref, wa_s, wb_s, wc_s):
    @pl.when(pl.program_id(1) == 0)
    def _():
        wa_s[...] = wa_ref[...].astype(BF16)
        wb_s[...] = wb_ref[...].astype(BF16)
        wc_s[...] = wc_ref[...].astype(BF16)

    pa = jnp.dot(ya_ref[...], wa_s[...], preferred_element_type=F32)
    pb = jnp.dot(yb_ref[...], wb_s[...], preferred_element_type=F32)
    pc = jnp.dot(yc_ref[...], wc_s[...], preferred_element_type=F32)
    merged = _sigmoid(g0_ref[...]) * pa + _sigmoid(g1_ref[...]) * pb + _sigmoid(g2_ref[...]) * pc
    o_ref[...] = merged.astype(o_ref.dtype)


def _merge(y_a, y_b, y_c, w_pa, w_pb, w_pc, layer, proj):
    S = y_a.shape[0]
    D = w_pa.shape[-1]
    tm = min(512, S)
    tn = 1024
    act = lambda width: pl.BlockSpec((tm, width), lambda j, i: (i, 0))
    wgt = lambda width: pl.BlockSpec((None, width, tn), lambda j, i: (layer, 0, j), pipeline_mode=pl.Buffered(1))
    gate = lambda b: pl.BlockSpec((tm, tn), lambda j, i: (i, (COL_GATE + b * D) // tn + j))
    return pl.pallas_call(
        _merge_body,
        grid=(D // tn, S // tm),
        in_specs=[act(MLA_WIDTH), act(SG_WIDTH), act(ML_WIDTH),
                  wgt(MLA_WIDTH), wgt(SG_WIDTH), wgt(ML_WIDTH),
                  gate(0), gate(1), gate(2)],
        out_specs=pl.BlockSpec((tm, tn), lambda j, i: (i, j)),
        out_shape=jax.ShapeDtypeStruct((S, D), BF16),
        scratch_shapes=[pltpu.VMEM((MLA_WIDTH, tn), BF16), pltpu.VMEM((SG_WIDTH, tn), BF16),
                        pltpu.VMEM((ML_WIDTH, tn), BF16)],
        compiler_params=_cparams(2),
        name="gated_merge",
    )(y_a, y_b, y_c, w_pa, w_pb, w_pc, proj, proj, proj)


def _outproj_body(m_ref, w_ref, x_ref, o_ref, w_s):
    @pl.when(pl.program_id(1) == 0)
    def _():
        w_s[...] = w_ref[...].astype(BF16)

    o_ref[...] = x_ref[...] + jnp.dot(m_ref[...], w_s[...], preferred_element_type=F32)


def _out_proj(merged, w_out, layer, x):
    S, D = x.shape
    tm = min(512, S)
    tn = 1024
    return pl.pallas_call(
        _outproj_body,
        grid=(D // tn, S // tm),
        in_specs=[pl.BlockSpec((tm, D), lambda j, i: (i, 0)),
                  pl.BlockSpec((None, D, tn), lambda j, i: (layer, 0, j), pipeline_mode=pl.Buffered(1)),
                  pl.BlockSpec((tm, tn), lambda j, i: (i, j))],
        out_specs=pl.BlockSpec((tm, tn), lambda j, i: (i, j)),
        out_shape=jax.ShapeDtypeStruct((S, D), F32),
        scratch_shapes=[pltpu.VMEM((D, tn), BF16)],
        compiler_params=_cparams(2),
        name="out_proj",
    )(merged, w_out, x)


_IN_SEGMENTS = (
    (0, Q_RANK, COL_CQ), (1024, KV_RANK, COL_CKV), (1536, MLA_ROPE, COL_KR), (1600, MLA_WIDTH, COL_ZA),
    (3648, SG_WIDTH, COL_SGU), (4672, SG_WIDTH, COL_SGV), (5696, SG_WIDTH, COL_ZB),
    (6720, ML_WIDTH, COL_MLX), (7744, ML_WIDTH, COL_MLV), (8768, ML_WIDTH, COL_MLO), (9792, ML_WIDTH, COL_ZC),
    (10816, ML_HEADS, COL_MLI), (10820, ML_HEADS, COL_MLF), (10824, 3 * D_MODEL, COL_GATE),
)
IN_WIDTH = 10824 + 3 * D_MODEL
IN_WIDTH_LANES = -(-IN_WIDTH // LANES) * LANES


def _pack_w_in_body(w_ref, o_ref):
    o_ref[:, COL_KR:IN_PAD] = jnp.zeros((o_ref.shape[0], IN_PAD - COL_KR), BF16)
    for src, width, dst in _IN_SEGMENTS:
        o_ref[:, dst:dst + width] = w_ref[:, src:src + width].astype(BF16)


def _pack_w_in(w_in_b, layer):
    D = w_in_b.shape[1]
    tr = 128
    return pl.pallas_call(
        _pack_w_in_body,
        grid=(D // tr,),
        in_specs=[pl.BlockSpec((None, tr, IN_WIDTH_LANES), lambda i: (layer, i, 0))],
        out_specs=pl.BlockSpec((tr, IN_PAD), lambda i: (i, 0)),
        out_shape=jax.ShapeDtypeStruct((D, IN_PAD), BF16),
        compiler_params=_cparams(1),
        name="pack_w_in",
    )(w_in_b)


def _pack_w_uq(w_uq):
    lead = w_uq.shape[:-1]
    w = w_uq.reshape(lead + (MLA_HEADS, MLA_NOPE + MLA_ROPE)).astype(BF16)
    w = jnp.pad(w, [(0, 0)] * len(lead) + [(0, 0), (0, MLA_QK_PAD - MLA_NOPE - MLA_ROPE)])
    return w.reshape(lead + (MLA_HEADS * MLA_QK_PAD,))


def _rope_tables(S):
    half = MLA_ROPE // 2
    inv_freq = ROPE_THETA ** (-jnp.arange(0, MLA_ROPE, 2, dtype=F32) / MLA_ROPE)
    ang = jnp.arange(S, dtype=F32)[:, None] * inv_freq[None, :]
    cos, sin = jnp.cos(ang), jnp.sin(ang)
    z = lambda n: jnp.zeros((S, n), F32)
    cosf = jnp.concatenate([cos, cos, z(LANES - MLA_ROPE)], axis=1)
    sin_lo = jnp.concatenate([-sin, z(LANES - half)], axis=1)
    sin_hi = jnp.concatenate([z(half), sin, z(LANES - MLA_ROPE)], axis=1)
    return cosf, sin_lo, sin_hi


def _layer(x, tables, layer, norm_g, w_in_p, mla_gq, mla_gkv, w_uq_p, w_ukv_b, sg_ln_g, sg_ln_b, sg_ws, sg_bs,
           ml_conv_w, ml_conv_b, ml_wq_b, ml_wk_b, ml_bi, ml_bf, ml_gn_g, w_pa, w_pb, w_pc, w_out):
    h = _rmsnorm(x, norm_g, BF16)
    proj = _in_proj(h, w_in_p)
    q = _q_proj(proj, mla_gq, w_uq_p, *tables)
    k, v = _kv_proj(proj, mla_gkv, w_ukv_b, *tables)
    y_a = _attention(q, k, v, proj)
    y_b = _spatial_gating(proj, sg_ln_g, sg_ln_b, sg_ws, sg_bs)
    y_c = _mlstm(proj, ml_conv_w, ml_conv_b, ml_wq_b, ml_wk_b, ml_bi, ml_bf, ml_gn_g)
    merged = _merge(y_a, y_b, y_c, w_pa, w_pb, w_pc, layer, proj)
    return _out_proj(merged, w_out, layer, x)


def kernel(x, norm_g, w_in, mla_gq, mla_gkv, mla_wuq, mla_wukv, sg_ln_g, sg_ln_b, sg_ws, sg_bs, ml_conv_w, ml_conv_b, ml_wq, ml_wk, ml_bi, ml_bf, ml_gn_g, w_pa, w_pb, w_pc, w_out, final_g):
    B, S, D = x.shape
    depth = w_in.shape[0]
    tables = _rope_tables(S)
    cast = lambda w: w.astype(BF16)
    w_in_b = jnp.pad(cast(w_in), ((0, 0), (0, 0), (0, IN_WIDTH_LANES - IN_WIDTH)))
    outs = []
    for b in range(B):
        xb = x[b]
        for l in range(depth):
            xb = _layer(xb, tables, l, norm_g[l], _pack_w_in(w_in_b, l), mla_gq[l], mla_gkv[l],
                        _pack_w_uq(mla_wuq[l]), cast(mla_wukv[l]),
                        sg_ln_g[l], sg_ln_b[l], sg_ws[l], sg_bs[l], ml_conv_w[l], ml_conv_b[l],
                        cast(ml_wq[l]), cast(ml_wk[l]), ml_bi[l], ml_bf[l], ml_gn_g[l],
                        w_pa, w_pb, w_pc, w_out)
        outs.append(_rmsnorm(xb, final_g, x.dtype))
    return outs[0][None] if B == 1 else jnp.stack(outs, axis=0)
```

```python
import functools

import jax
import jax.numpy as jnp
from jax import lax
from jax.experimental import pallas as pl
from jax.experimental.pallas import tpu as pltpu

F32 = jnp.float32
BF16 = jnp.bfloat16

D_MODEL = 4096
CHUNK = 64
EPS = 1e-6

MLA_HEADS = 16
MLA_NOPE = 128
MLA_ROPE = 64
MLA_V = 128
Q_RANK = 1024
KV_RANK = 512
ROPE_THETA = 10000.0
MLA_WIDTH = MLA_HEADS * MLA_V
MLA_QK_PAD = 256

SG_BLOCK = 128
SG_GROUPS = 4
SG_WIDTH = 1024
SG_GROUP_DIM = SG_WIDTH // SG_GROUPS

ML_HEADS = 4
ML_WIDTH = 1024
ML_HEAD_DIM = 256
CONV_K = 4
ML_CHUNK = 128

LANES = 128
NEG = -1e30

COL_CQ = 0
COL_ZA = 1024
COL_SGU = 3072
COL_SGV = 4096
COL_ZB = 5120
COL_MLX = 6144
COL_MLV = 7168
COL_MLO = 8192
COL_ZC = 9216
COL_GATE = 10240
COL_CKV = 22528
COL_KR = 23040
COL_MLI = 23168
IN_PAD = 23552

VMEM_LIMIT = 56 * 1024 * 1024


def _cparams(n_axes, vmem=VMEM_LIMIT):
    return pltpu.CompilerParams(dimension_semantics=("arbitrary",) * n_axes,
                                vmem_limit_bytes=vmem)


def _sigmoid(x):
    return 1.0 / (1.0 + jnp.exp(-x))


def _silu(x):
    return x * _sigmoid(x)


def _rmsnorm_body(x_ref, g_ref, o_ref):
    x = x_ref[...]
    y = x * lax.rsqrt(jnp.mean(x * x, axis=-1, keepdims=True) + EPS)
    o_ref[...] = (y * g_ref[...]).astype(o_ref.dtype)


def _rmsnorm(x, g, out_dtype):
    S, D = x.shape
    tm = min(256, S)
    return pl.pallas_call(
        _rmsnorm_body,
        grid=(S // tm,),
        in_specs=[pl.BlockSpec((tm, D), lambda i: (i, 0)),
                  pl.BlockSpec((1, D), lambda i: (0, 0))],
        out_specs=pl.BlockSpec((tm, D), lambda i: (i, 0)),
        out_shape=jax.ShapeDtypeStruct((S, D), out_dtype),
        compiler_params=_cparams(1),
        name="rmsnorm",
    )(x, g.reshape(1, D))


IN_TN = 1024
IN_CHUNK = 128
IN_CHUNKS_PER_TILE = IN_TN // IN_CHUNK

_IN_TILE_SRC = (0, 1600, 2624, 3648, 4672, 5696, 6720, 7744, 8768, 9792) + tuple(
    10824 + IN_TN * t for t in range(3 * D_MODEL // IN_TN))
_IN_CHUNK_SRC = tuple(src + IN_CHUNK * c for src in _IN_TILE_SRC for c in range(IN_CHUNKS_PER_TILE)) + (
    1024, 1152, 1280, 1408, 1536, 10816, 0, 0)
assert len(_IN_CHUNK_SRC) * IN_CHUNK == IN_PAD


def _in_proj_body(src_ref, h_ref, wt_hbm, o_ref, stage, wb, sem, *, layer, chunks_per_step):
    j = pl.program_id(0)
    i = pl.program_id(1)
    slot = lax.rem(j, 2)

    def chunk_copy(tile, c, buf):
        row = pl.multiple_of(src_ref[tile * IN_CHUNKS_PER_TILE + c], 8)
        return pltpu.make_async_copy(wt_hbm.at[layer, pl.ds(row, IN_CHUNK), :], stage.at[buf], sem.at[buf])

    def start_fetch(tile, chunks):
        chunk_copy(tile, chunks[0], 0).start()

    def finish_fetch(tile, dst_slot, chunks):
        for n, c in enumerate(chunks):
            if n + 1 < len(chunks):
                chunk_copy(tile, chunks[n + 1], (n + 1) % 2).start()
            chunk_copy(tile, c, n % 2).wait()
            start = c * IN_CHUNK if isinstance(c, int) else pl.multiple_of(c * IN_CHUNK, IN_CHUNK)
            wb[dst_slot, pl.ds(start, IN_CHUNK), :] = stage[n % 2].astype(BF16)

    @pl.when(jnp.logical_and(j == 0, i == 0))
    def _():
        first_tile = list(range(IN_CHUNKS_PER_TILE))
        start_fetch(0, first_tile)
        finish_fetch(0, 0, first_tile)

    prefetch = jnp.logical_and(j + 1 < pl.num_programs(0), i * chunks_per_step < IN_CHUNKS_PER_TILE)
    share = [i * chunks_per_step + n for n in range(chunks_per_step)]

    @pl.when(prefetch)
    def _():
        start_fetch(j + 1, share)

    o_ref[...] = lax.dot_general(h_ref[...], wb[slot], (((1,), (1,)), ((), ())), preferred_element_type=F32)

    @pl.when(prefetch)
    def _():
        finish_fetch(j + 1, 1 - slot, share)


def _in_proj(h, w_in_t, layer):
    S, K = h.shape
    tm = min(1024, S)
    n_row_tiles = S // tm
    chunks_per_step = -(-IN_CHUNKS_PER_TILE // n_row_tiles)
    assert IN_CHUNKS_PER_TILE % chunks_per_step == 0
    grid_spec = pltpu.PrefetchScalarGridSpec(
        num_scalar_prefetch=1,
        grid=(IN_PAD // IN_TN, n_row_tiles),
        in_specs=[pl.BlockSpec((tm, K), lambda j, i, src: (i, 0)),
                  pl.BlockSpec(memory_space=pl.ANY)],
        out_specs=pl.BlockSpec((tm, IN_TN), lambda j, i, src: (i, j)),
        scratch_shapes=[pltpu.VMEM((2, IN_CHUNK, K), F32), pltpu.VMEM((2, IN_TN, K), BF16),
                        pltpu.SemaphoreType.DMA((2,))])
    return pl.pallas_call(
        functools.partial(_in_proj_body, layer=layer, chunks_per_step=chunks_per_step),
        grid_spec=grid_spec,
        out_shape=jax.ShapeDtypeStruct((S, IN_PAD), F32),
        compiler_params=_cparams(2),
        name="in_proj",
    )(jnp.asarray(_IN_CHUNK_SRC, jnp.int32), h, w_in_t)


def _rope(seg, cosf, sin_lo, sin_hi):
    return (seg * cosf
            + pltpu.roll(seg, LANES - MLA_ROPE // 2, 1) * sin_lo
            + pltpu.roll(seg, MLA_ROPE // 2, 1) * sin_hi)


def _qproj_body(cq_ref, g_ref, w_ref, cos_ref, slo_ref, shi_ref, o_ref, a_scr):
    @pl.when(pl.program_id(1) == 0)
    def _():
        c = cq_ref[...]
        y = c * lax.rsqrt(jnp.mean(c * c, axis=-1, keepdims=True) + EPS)
        a_scr[...] = (y * g_ref[...]).astype(BF16)

    res = jnp.dot(a_scr[...], w_ref[...], preferred_element_type=F32) * ATTN_EXP2_SCALE
    cosf, slo, shi = cos_ref[...], slo_ref[...], shi_ref[...]
    for hh in range(o_ref.shape[0]):
        base = hh * MLA_QK_PAD
        o_ref[hh, :, 0:MLA_NOPE] = res[:, base:base + MLA_NOPE].astype(BF16)
        o_ref[hh, :, MLA_NOPE:MLA_QK_PAD] = _rope(
            res[:, base + MLA_NOPE:base + MLA_QK_PAD], cosf, slo, shi).astype(BF16)


def _q_proj(proj, g, w, cosf, slo, shi):
    S = proj.shape[0]
    tm = min(1024, S)
    hpb = 4
    tn = hpb * MLA_QK_PAD
    row = lambda i, j: (i, 0)
    return pl.pallas_call(
        _qproj_body,
        grid=(S // tm, MLA_HEADS // hpb),
        in_specs=[pl.BlockSpec((tm, Q_RANK), lambda i, j: (i, COL_CQ // Q_RANK)),
                  pl.BlockSpec((1, Q_RANK), lambda i, j: (0, 0)),
                  pl.BlockSpec((Q_RANK, tn), lambda i, j: (0, j)),
                  pl.BlockSpec((tm, LANES), row),
                  pl.BlockSpec((tm, LANES), row),
                  pl.BlockSpec((tm, LANES), row)],
        out_specs=pl.BlockSpec((hpb, tm, MLA_QK_PAD), lambda i, j: (j, i, 0)),
        out_shape=jax.ShapeDtypeStruct((MLA_HEADS, S, MLA_QK_PAD), BF16),
        scratch_shapes=[pltpu.VMEM((tm, Q_RANK), BF16)],
        compiler_params=_cparams(2),
        name="mla_q_proj",
    )(proj, g.reshape(1, Q_RANK), w, cosf, slo, shi)


def _kvproj_body(ckv_ref, kr_ref, g_ref, w_ref, cos_ref, slo_ref, shi_ref, k_ref, v_ref, a_scr):
    @pl.when(pl.program_id(1) == 0)
    def _():
        c = ckv_ref[...]
        y = c * lax.rsqrt(jnp.mean(c * c, axis=-1, keepdims=True) + EPS)
        a_scr[...] = (y * g_ref[...]).astype(BF16)

    res = jnp.dot(a_scr[...], w_ref[...], preferred_element_type=F32)
    k_rot = _rope(kr_ref[...], cos_ref[...], slo_ref[...], shi_ref[...]).astype(BF16)
    width = MLA_NOPE + MLA_V
    for hh in range(k_ref.shape[0]):
        base = hh * width
        k_ref[hh, :, 0:MLA_NOPE] = res[:, base:base + MLA_NOPE].astype(BF16)
        k_ref[hh, :, MLA_NOPE:MLA_QK_PAD] = k_rot
        v_ref[hh] = res[:, base + MLA_NOPE:base + width].astype(BF16)


def _kv_proj(proj, g, w, cosf, slo, shi):
    S = proj.shape[0]
    tm = min(1024, S)
    hpb = 4
    tn = hpb * (MLA_NOPE + MLA_V)
    row = lambda i, j: (i, 0)
    return pl.pallas_call(
        _kvproj_body,
        grid=(S // tm, MLA_HEADS // hpb),
        in_specs=[pl.BlockSpec((tm, KV_RANK), lambda i, j: (i, COL_CKV // KV_RANK)),
                  pl.BlockSpec((tm, LANES), lambda i, j: (i, COL_KR // LANES)),
                  pl.BlockSpec((1, KV_RANK), lambda i, j: (0, 0)),
                  pl.BlockSpec((KV_RANK, tn), lambda i, j: (0, j)),
                  pl.BlockSpec((tm, LANES), row),
                  pl.BlockSpec((tm, LANES), row),
                  pl.BlockSpec((tm, LANES), row)],
        out_specs=[pl.BlockSpec((hpb, tm, MLA_QK_PAD), lambda i, j: (j, i, 0)),
                   pl.BlockSpec((hpb, tm, MLA_V), lambda i, j: (j, i, 0))],
        out_shape=[jax.ShapeDtypeStruct((MLA_HEADS, S, MLA_QK_PAD), BF16),
                   jax.ShapeDtypeStruct((MLA_HEADS, S, MLA_V), BF16)],
        scratch_shapes=[pltpu.VMEM((tm, KV_RANK), BF16)],
        compiler_params=_cparams(2),
        name="mla_kv_proj",
    )(proj, proj, g.reshape(1, KV_RANK), w, cosf, slo, shi)


ATTN_HEADS_PER_STEP = 2
ATTN_EXP2_SCALE = float(MLA_NOPE + MLA_ROPE) ** -0.5 * 1.4426950408889634
ATTN_STRIP = 64


def _attn_body(q_ref, k_ref, v_ref, z_ref, o_ref, s_scr, p_scr, m_scr, a_scr, acc_scr, *, blk):
    i = pl.program_id(1)
    A, B = 0, 1
    ones_col = (lax.broadcasted_iota(jnp.int32, (blk, MLA_V), 1) == 0).astype(BF16)
    col_chunk = lax.broadcasted_iota(jnp.int32, (ATTN_STRIP, blk), 1) // CHUNK
    row_in_strip = lax.broadcasted_iota(jnp.int32, (ATTN_STRIP, blk), 0)

    def score(h, kb):
        k = k_ref[h, pl.ds(pl.multiple_of(kb * blk, blk), blk), :]
        s_scr[h] = lax.dot_general(q_ref[h], k, (((1,), (1,)), ((), ())), preferred_element_type=F32)

    def softmax(h, diagonal):
        for r in range(blk // ATTN_STRIP):
            rows = slice(r * ATTN_STRIP, (r + 1) * ATTN_STRIP)
            s = s_scr[h, rows, :]
            if diagonal:
                s = jnp.where(col_chunk <= (row_in_strip + r * ATTN_STRIP) // CHUNK, s, NEG)
                m_new = jnp.broadcast_to(jnp.max(s, axis=-1, keepdims=True), (ATTN_STRIP, LANES))
                a_scr[h, rows, :] = jnp.ones((ATTN_STRIP, LANES), F32)
            else:
                m_old = m_scr[h, rows, :]
                m_new = jnp.maximum(m_old, jnp.max(s, axis=-1, keepdims=True))
                a_scr[h, rows, :] = jnp.exp2(m_old - m_new)
            m_scr[h, rows, :] = m_new
            p_scr[h, rows, :] = jnp.exp2(s - jnp.concatenate([m_new] * (blk // LANES), axis=1)).astype(BF16)

    def apply_values(h, kb):
        v = v_ref[h, pl.ds(pl.multiple_of(kb * blk, blk), blk), :]
        pv = jnp.dot(p_scr[h], jnp.concatenate([v, ones_col], axis=1), preferred_element_type=F32)
        a = a_scr[h]
        acc_scr[h] = jnp.concatenate([a, a], axis=1) * acc_scr[h] + pv

    acc_scr[...] = jnp.zeros_like(acc_scr)
    score(A, i)
    softmax(A, True)
    score(B, i)
    softmax(B, True)
    score(A, 0)
    apply_values(A, i)

    def step(j):
        softmax(A, False)
        score(B, j - 1)
        apply_values(B, jnp.where(j == 1, i, j - 2))
        softmax(B, False)
        score(A, j)
        apply_values(A, j - 1)

    def two_steps(t, carry):
        step(2 * t + 1)
        step(2 * t + 2)
        return carry

    lax.fori_loop(0, lax.shift_right_logical(i, 1), two_steps, 0)

    @pl.when(lax.bitwise_and(i, 1) == 1)
    def _():
        step(i)

    apply_values(B, jnp.maximum(i - 1, 0))
    for h in (A, B):
        cols = slice(h * MLA_V, (h + 1) * MLA_V)
        attn = acc_scr[h, :, 0:MLA_V] / acc_scr[h, :, MLA_V:MLA_V + 1]
        o_ref[:, cols] = (attn * _silu(z_ref[:, cols])).astype(o_ref.dtype)


def _attention(q, k, v, proj):
    H, S, _ = q.shape
    blk = min(512, S)
    hp = ATTN_HEADS_PER_STEP
    assert hp == 2
    return pl.pallas_call(
        functools.partial(_attn_body, blk=blk),
        grid=(H // hp, S // blk),
        in_specs=[pl.BlockSpec((hp, blk, MLA_QK_PAD), lambda h, i: (h, i, 0)),
                  pl.BlockSpec((hp, S, MLA_QK_PAD), lambda h, i: (h, 0, 0)),
                  pl.BlockSpec((hp, S, MLA_V), lambda h, i: (h, 0, 0)),
                  pl.BlockSpec((blk, hp * MLA_V), lambda h, i: (i, COL_ZA // (hp * MLA_V) + h))],
        out_specs=pl.BlockSpec((blk, hp * MLA_V), lambda h, i: (i, h)),
        out_shape=jax.ShapeDtypeStruct((S, MLA_WIDTH), BF16),
        scratch_shapes=[pltpu.VMEM((hp, blk, blk), F32), pltpu.VMEM((hp, blk, blk), BF16),
                        pltpu.VMEM((hp, blk, LANES), F32), pltpu.VMEM((hp, blk, LANES), F32),
                        pltpu.VMEM((hp, blk, 2 * MLA_V), F32)],
        compiler_params=_cparams(2),
        name="mla_attention",
    )(q, k, v, proj)


def _sgu_body(u_ref, v_ref, z_ref, lng_ref, lnb_ref, ws_ref, bst_ref, o_ref):
    v = v_ref[...]
    mu = jnp.mean(v, axis=-1, keepdims=True)
    vc = v - mu
    var = jnp.mean(vc * vc, axis=-1, keepdims=True)
    vn = (vc * lax.rsqrt(var + EPS) * lng_ref[...] + lnb_ref[...]).astype(BF16)
    out_chunk = lax.broadcasted_iota(jnp.int32, (SG_BLOCK, SG_BLOCK), 0) // CHUNK
    in_chunk = lax.broadcasted_iota(jnp.int32, (SG_BLOCK, SG_BLOCK), 1) // CHUNK
    causal = in_chunk <= out_chunk
    bst = bst_ref[...]
    for g in range(SG_GROUPS):
        w = jnp.where(causal, ws_ref[g], 0.0).astype(BF16)
        cols = slice(g * SG_GROUP_DIM, (g + 1) * SG_GROUP_DIM)
        bias = bst[:, g:g + 1]
        for b in range(u_ref.shape[0] // SG_BLOCK):
            rows = slice(b * SG_BLOCK, (b + 1) * SG_BLOCK)
            s = jnp.dot(w, vn[rows, cols], preferred_element_type=F32) + bias
            o_ref[rows, cols] = (u_ref[rows, cols] * s * _silu(z_ref[rows, cols])).astype(o_ref.dtype)


def _spatial_gating(proj, ln_g, ln_b, w_s, b_s):
    S = proj.shape[0]
    tm = min(512, S)
    blk = lambda c: pl.BlockSpec((tm, SG_WIDTH), lambda i: (i, c // SG_WIDTH))
    full = lambda shape: pl.BlockSpec(shape, lambda i: (0,) * len(shape))
    return pl.pallas_call(
        _sgu_body,
        grid=(S // tm,),
        in_specs=[blk(COL_SGU), blk(COL_SGV), blk(COL_ZB),
                  full((1, SG_WIDTH)), full((1, SG_WIDTH)),
                  full((SG_GROUPS, SG_BLOCK, SG_BLOCK)), full((SG_BLOCK, SG_GROUPS))],
        out_specs=pl.BlockSpec((tm, SG_WIDTH), lambda i: (i, 0)),
        out_shape=jax.ShapeDtypeStruct((S, SG_WIDTH), BF16),
        compiler_params=_cparams(1),
        name="spatial_gating",
    )(proj, proj, proj, ln_g.reshape(1, SG_WIDTH), ln_b.reshape(1, SG_WIDTH), w_s, jnp.transpose(b_s))


def _mlstm_body(x_ref, v_ref, og_ref, z_ref, if_ref, cw_ref, cb_ref, wq_ref, wk_ref,
                bi_ref, bf_ref, gn_ref, o_ref, xs_scr, ct_scr, n_scr, m_scr, h_scr, *, T, L):
    halo = 8

    @pl.when(pl.program_id(0) == 0)
    def _():
        xs_scr[0:halo, :] = jnp.zeros((halo, ML_WIDTH), F32)
        ct_scr[...] = jnp.zeros_like(ct_scr)
        n_scr[...] = jnp.zeros_like(n_scr)
        m_scr[...] = jnp.zeros_like(m_scr)

    xs_scr[halo:halo + T, :] = x_ref[...]
    cw = cw_ref[...]
    xc = cb_ref[...]
    for kk in range(CONV_K):
        off = halo - (CONV_K - 1) + kk
        xc = xc + cw[kk:kk + 1, :] * xs_scr[off:off + T, :]
    xs_scr[0:halo, :] = xs_scr[T:T + halo, :]
    xcb = _silu(xc).astype(BF16)

    gates = if_ref[...]
    head_lane = lax.broadcasted_iota(jnp.int32, gates.shape, 1) < ML_HEADS
    ig = jnp.where(head_lane, gates + bi_ref[...], 0.0)
    fg = jnp.where(head_lane, pltpu.roll(gates, LANES - ML_HEADS, 1) + bf_ref[...], 0.0)
    lf = jnp.minimum(fg, 0.0) - jnp.log1p(jnp.exp(-jnp.abs(fg)))
    rr = lax.broadcasted_iota(jnp.int32, (L, L), 0)
    cc = lax.broadcasted_iota(jnp.int32, (L, L), 1)
    tril = cc <= rr
    tri_f = tril.astype(F32)
    nc = T // L
    g_l, G_l, a_l, e_l, r_l = [], [], [], [], []
    for c in range(nc):
        rows = slice(c * L, (c + 1) * L)
        g_c = jnp.dot(tri_f, lf[rows], precision=lax.Precision.HIGHEST, preferred_element_type=F32)
        G_c = g_c[L - 1:L, :]
        w_end = G_c - g_c + ig[rows]
        a_c = jnp.max(w_end, axis=0, keepdims=True)
        g_l.append(g_c)
        G_l.append(G_c)
        a_l.append(a_c)
        e_l.append(jnp.exp(w_end - a_c))
        r_l.append(ig[rows] - g_c)
    r_t = jnp.transpose(jnp.concatenate(r_l, axis=0))

    q_l, k_l = [], []
    for h in range(ML_HEADS):
        cols = slice(h * ML_HEAD_DIM, (h + 1) * ML_HEAD_DIM)
        q_l.append(jnp.dot(xcb[:, cols], wq_ref[h], preferred_element_type=F32))
        k_l.append(jnp.dot(xcb[:, cols], wk_ref[h], preferred_element_type=F32) * (ML_HEAD_DIM ** -0.5))

    m_vec = m_scr[0:1, :]
    for c in range(nc):
        rows = slice(c * L, (c + 1) * L)
        g_c, G_c, a_c, e_c = g_l[c], G_l[c], a_l[c], e_l[c]
        m_new = jnp.maximum(G_c + m_vec, a_c)
        sp = jnp.exp(G_c + m_vec - m_new)
        sl = jnp.exp(a_c - m_new)
        for h in range(ML_HEADS):
            cols = slice(h * ML_HEAD_DIM, (h + 1) * ML_HEAD_DIM)
            q_c = q_l[h][rows]
            k_c = k_l[h][rows]
            v_c = v_ref[rows, cols]
            q_cb = q_c.astype(BF16)
            g_col = g_c[:, h:h + 1]
            d = jnp.where(tril, g_col + r_t[h:h + 1, rows], NEG)
            inter_log = g_col + m_vec[:, h:h + 1]
            m_i = jnp.maximum(jnp.max(d, axis=1, keepdims=True), inter_log)
            p = jnp.exp(d - m_i)
            qk = lax.dot_general(q_cb, k_c.astype(BF16), (((1,), (1,)), ((), ())),
                                 preferred_element_type=F32)
            sm = qk * p
            inter_scale = jnp.exp(inter_log - m_i)
            ct = ct_scr[h]
            n_prev = n_scr[h:h + 1, :]
            num = (jnp.dot(sm.astype(BF16), v_c.astype(BF16), preferred_element_type=F32)
                   + inter_scale * jnp.dot(q_cb, ct.astype(BF16), preferred_element_type=F32))
            den = (jnp.sum(sm, axis=1, keepdims=True)
                   + inter_scale * jnp.sum(q_c * n_prev, axis=1, keepdims=True))
            h_scr[rows, cols] = num / jnp.maximum(jnp.abs(den), jnp.exp(-m_i))

            e_col = e_c[:, h:h + 1]
            ev = (e_col * v_c).astype(BF16)
            c_loc_t = jnp.dot(jnp.transpose(k_c).astype(BF16), ev, preferred_element_type=F32)
            sp_h = sp[:, h:h + 1]
            sl_h = sl[:, h:h + 1]
            ct_scr[h] = sp_h * ct + sl_h * c_loc_t
            n_scr[h:h + 1, :] = sp_h * n_prev + sl_h * jnp.sum(e_col * k_c, axis=0, keepdims=True)
        m_vec = m_new
    m_scr[0:1, :] = m_vec

    for h in range(ML_HEADS):
        cols = slice(h * ML_HEAD_DIM, (h + 1) * ML_HEAD_DIM)
        hs = h_scr[:, cols] * _sigmoid(og_ref[:, cols])
        mu = jnp.mean(hs, axis=-1, keepdims=True)
        hc = hs - mu
        var = jnp.mean(hc * hc, axis=-1, keepdims=True)
        y = hc * lax.rsqrt(var + EPS) * gn_ref[:, cols]
        o_ref[:, cols] = (y * _silu(z_ref[:, cols])).astype(o_ref.dtype)


def _mlstm(proj, conv_w, conv_b, w_q, w_k, b_i, b_f, gn_g):
    S = proj.shape[0]
    T = min(512, S)
    L = ML_CHUNK
    wide = lambda c: pl.BlockSpec((T, ML_WIDTH), lambda i: (i, c // ML_WIDTH))
    gate = lambda c: pl.BlockSpec((T, LANES), lambda i: (i, c // LANES))
    full = lambda shape: pl.BlockSpec(shape, lambda i: (0,) * len(shape))
    pad_gate_bias = lambda b: jnp.pad(b, (0, LANES - ML_HEADS)).reshape(1, LANES)
    return pl.pallas_call(
        functools.partial(_mlstm_body, T=T, L=L),
        grid=(S // T,),
        in_specs=[wide(COL_MLX), wide(COL_MLV), wide(COL_MLO), wide(COL_ZC),
                  gate(COL_MLI),
                  full((CONV_K, ML_WIDTH)), full((1, ML_WIDTH)),
                  full((ML_HEADS, ML_HEAD_DIM, ML_HEAD_DIM)), full((ML_HEADS, ML_HEAD_DIM, ML_HEAD_DIM)),
                  full((1, LANES)), full((1, LANES)), full((1, ML_WIDTH))],
        out_specs=pl.BlockSpec((T, ML_WIDTH), lambda i: (i, 0)),
        out_shape=jax.ShapeDtypeStruct((S, ML_WIDTH), BF16),
        scratch_shapes=[pltpu.VMEM((T + 8, ML_WIDTH), F32),
                        pltpu.VMEM((ML_HEADS, ML_HEAD_DIM, ML_HEAD_DIM), F32),
                        pltpu.VMEM((8, ML_HEAD_DIM), F32),
                        pltpu.VMEM((8, LANES), F32),
                        pltpu.VMEM((T, ML_WIDTH), F32)],
        compiler_params=_cparams(1),
        name="mlstm",
    )(proj, proj, proj, proj, proj, conv_w, conv_b.reshape(1, ML_WIDTH), w_q, w_k,
      pad_gate_bias(b_i), pad_gate_bias(b_f), gn_g.reshape(1, ML_WIDTH))


def _merge_body(ya_ref, yb_ref, yc_ref, wa_ref, wb_ref, wc_ref, g0_ref, g1_ref, g2_ref, o_ref, wa_s, wb_s, wc_s):
    @pl.when(pl.program_id(1) == 0)
    def _():
        wa_s[...] = wa_ref[...].astype(BF16)
        wb_s[...] = wb_ref[...].astype(BF16)
        wc_s[...] = wc_ref[...].astype(BF16)

    pa = jnp.dot(ya_ref[...], wa_s[...], preferred_element_type=F32)
    pb = jnp.dot(yb_ref[...], wb_s[...], preferred_element_type=F32)
    pc = jnp.dot(yc_ref[...], wc_s[...], preferred_element_type=F32)
    merged = _sigmoid(g0_ref[...]) * pa + _sigmoid(g1_ref[...]) * pb + _sigmoid(g2_ref[...]) * pc
    o_ref[...] = merged.astype(o_ref.dtype)


def _merge(y_a, y_b, y_c, w_pa, w_pb, w_pc, layer, proj):
    S = y_a.shape[0]
    D = w_pa.shape[-1]
    tm = min(512, S)
    tn = 1024
    act = lambda width: pl.BlockSpec((tm, width), lambda j, i: (i, 0))
    wgt = lambda width: pl.BlockSpec((None, width, tn), lambda j, i: (layer, 0, j), pipeline_mode=pl.Buffered(1))
    gate = lambda b: pl.BlockSpec((tm, tn), lambda j, i: (i, (COL_GATE + b * D) // tn + j))
    return pl.pallas_call(
        _merge_body,
        grid=(D // tn, S // tm),
        in_specs=[act(MLA_WIDTH), act(SG_WIDTH), act(ML_WIDTH),
                  wgt(MLA_WIDTH), wgt(SG_WIDTH), wgt(ML_WIDTH),
                  gate(0), gate(1), gate(2)],
        out_specs=pl.BlockSpec((tm, tn), lambda j, i: (i, j)),
        out_shape=jax.ShapeDtypeStruct((S, D), BF16),
        scratch_shapes=[pltpu.VMEM((MLA_WIDTH, tn), BF16), pltpu.VMEM((SG_WIDTH, tn), BF16),
                        pltpu.VMEM((ML_WIDTH, tn), BF16)],
        compiler_params=_cparams(2),
        name="gated_merge",
    )(y_a, y_b, y_c, w_pa, w_pb, w_pc, proj, proj, proj)


def _outproj_body(m_ref, w_ref, x_ref, o_ref, w_s):
    @pl.when(pl.program_id(1) == 0)
    def _():
        w_s[...] = w_ref[...].astype(BF16)

    o_ref[...] = x_ref[...] + jnp.dot(m_ref[...], w_s[...], preferred_element_type=F32)


def _out_proj(merged, w_out, layer, x):
    S, D = x.shape
    tm = min(512, S)
    tn = 1024
    return pl.pallas_call(
        _outproj_body,
        grid=(D // tn, S // tm),
        in_specs=[pl.BlockSpec((tm, D), lambda j, i: (i, 0)),
                  pl.BlockSpec((None, D, tn), lambda j, i: (layer, 0, j), pipeline_mode=pl.Buffered(1)),
                  pl.BlockSpec((tm, tn), lambda j, i: (i, j))],
        out_specs=pl.BlockSpec((tm, tn), lambda j, i: (i, j)),
        out_shape=jax.ShapeDtypeStruct((S, D), F32),
        scratch_shapes=[pltpu.VMEM((D, tn), BF16)],
        compiler_params=_cparams(2),
        name="out_proj",
    )(merged, w_out, x)


def _pack_w_uq(w_uq):
    lead = w_uq.shape[:-1]
    w = w_uq.reshape(lead + (MLA_HEADS, MLA_NOPE + MLA_ROPE)).astype(BF16)
    w = jnp.pad(w, [(0, 0)] * len(lead) + [(0, 0), (0, MLA_QK_PAD - MLA_NOPE - MLA_ROPE)])
    return w.reshape(lead + (MLA_HEADS * MLA_QK_PAD,))


def _rope_tables(S):
    half = MLA_ROPE // 2
    inv_freq = ROPE_THETA ** (-jnp.arange(0, MLA_ROPE, 2, dtype=F32) / MLA_ROPE)
    ang = jnp.arange(S, dtype=F32)[:, None] * inv_freq[None, :]
    cos, sin = jnp.cos(ang), jnp.sin(ang)
    z = lambda n: jnp.zeros((S, n), F32)
    cosf = jnp.concatenate([cos, cos, z(LANES - MLA_ROPE)], axis=1)
    sin_lo = jnp.concatenate([-sin, z(LANES - half)], axis=1)
    sin_hi = jnp.concatenate([z(half), sin, z(LANES - MLA_ROPE)], axis=1)
    return cosf, sin_lo, sin_hi


def _layer(x, tables, layer, norm_g, w_in_t, mla_gq, mla_gkv, w_uq_p, w_ukv_b, sg_ln_g, sg_ln_b, sg_ws, sg_bs,
           ml_conv_w, ml_conv_b, ml_wq_b, ml_wk_b, ml_bi, ml_bf, ml_gn_g, w_pa, w_pb, w_pc, w_out):
    h = _rmsnorm(x, norm_g, BF16)
    proj = _in_proj(h, w_in_t, layer)
    q = _q_proj(proj, mla_gq, w_uq_p, *tables)
    k, v = _kv_proj(proj, mla_gkv, w_ukv_b, *tables)
    y_a = _attention(q, k, v, proj)
    y_b = _spatial_gating(proj, sg_ln_g, sg_ln_b, sg_ws, sg_bs)
    y_c = _mlstm(proj, ml_conv_w, ml_conv_b, ml_wq_b, ml_wk_b, ml_bi, ml_bf, ml_gn_g)
    merged = _merge(y_a, y_b, y_c, w_pa, w_pb, w_pc, layer, proj)
    return _out_proj(merged, w_out, layer, x)


def kernel(x, norm_g, w_in, mla_gq, mla_gkv, mla_wuq, mla_wukv, sg_ln_g, sg_ln_b, sg_ws, sg_bs, ml_conv_w, ml_conv_b, ml_wq, ml_wk, ml_bi, ml_bf, ml_gn_g, w_pa, w_pb, w_pc, w_out, final_g):
    B, S, D = x.shape
    depth = w_in.shape[0]
    tables = _rope_tables(S)
    cast = lambda w: w.astype(BF16)
    w_in_t = jnp.swapaxes(w_in, 1, 2)
    outs = []
    for b in range(B):
        xb = x[b]
        for l in range(depth):
            xb = _layer(xb, tables, l, norm_g[l], w_in_t, mla_gq[l], mla_gkv[l],
                        _pack_w_uq(mla_wuq[l]), cast(mla_wukv[l]),
                        sg_ln_g[l], sg_ln_b[l], sg_ws[l], sg_bs[l], ml_conv_w[l], ml_conv_b[l],
                        cast(ml_wq[l]), cast(ml_wk[l]), ml_bi[l], ml_bf[l], ml_gn_g[l],
                        w_pa, w_pb, w_pc, w_out)
        outs.append(_rmsnorm(xb, final_g, x.dtype))
    return outs[0][None] if B == 1 else jnp.stack(outs, axis=0)
```

```python
import functools

import jax
import jax.numpy as jnp
from jax import lax
from jax.experimental import pallas as pl
from jax.experimental.pallas import tpu as pltpu

F32 = jnp.float32
BF16 = jnp.bfloat16

D_MODEL = 4096
CHUNK = 64
EPS = 1e-6

MLA_HEADS = 16
MLA_NOPE = 128
MLA_ROPE = 64
MLA_V = 128
Q_RANK = 1024
KV_RANK = 512
ROPE_THETA = 10000.0
MLA_WIDTH = MLA_HEADS * MLA_V
MLA_QK_PAD = 256

SG_BLOCK = 128
SG_GROUPS = 4
SG_WIDTH = 1024
SG_GROUP_DIM = SG_WIDTH // SG_GROUPS

ML_HEADS = 4
ML_WIDTH = 1024
ML_HEAD_DIM = 256
CONV_K = 4
ML_CHUNK = 128

LANES = 128
NEG = -1e30

COL_CQ = 0
COL_ZA = 1024
COL_SGU = 3072
COL_SGV = 4096
COL_ZB = 5120
COL_MLX = 6144
COL_MLV = 7168
COL_MLO = 8192
COL_ZC = 9216
COL_GATE = 10240
COL_CKV = 22528
COL_KR = 23040
COL_MLI = 23168
IN_PAD = 23552

VMEM_LIMIT = 56 * 1024 * 1024


def _cparams(n_axes, vmem=VMEM_LIMIT):
    return pltpu.CompilerParams(dimension_semantics=("arbitrary",) * n_axes,
                                vmem_limit_bytes=vmem)


def _sigmoid(x):
    return 1.0 / (1.0 + jnp.exp(-x))


def _silu(x):
    return x * _sigmoid(x)


def _rmsnorm_body(x_ref, g_ref, o_ref):
    x = x_ref[...]
    y = x * lax.rsqrt(jnp.mean(x * x, axis=-1, keepdims=True) + EPS)
    o_ref[...] = (y * g_ref[...]).astype(o_ref.dtype)


def _rmsnorm(x, g, out_dtype):
    S, D = x.shape
    tm = min(256, S)
    return pl.pallas_call(
        _rmsnorm_body,
        grid=(S // tm,),
        in_specs=[pl.BlockSpec((tm, D), lambda i: (i, 0)),
                  pl.BlockSpec((1, D), lambda i: (0, 0))],
        out_specs=pl.BlockSpec((tm, D), lambda i: (i, 0)),
        out_shape=jax.ShapeDtypeStruct((S, D), out_dtype),
        compiler_params=_cparams(1),
        name="rmsnorm",
    )(x, g.reshape(1, D))


IN_TN = 1024
IN_CHUNK = 128
IN_CHUNKS_PER_TILE = IN_TN // IN_CHUNK

_IN_TILE_SRC = (0, 1600, 2624, 3648, 4672, 5696, 6720, 7744, 8768, 9792) + tuple(
    10824 + IN_TN * t for t in range(3 * D_MODEL // IN_TN))
_IN_CHUNK_SRC = tuple(src + IN_CHUNK * c for src in _IN_TILE_SRC for c in range(IN_CHUNKS_PER_TILE)) + (
    1024, 1152, 1280, 1408, 1536, 10816, 0, 0)
assert len(_IN_CHUNK_SRC) * IN_CHUNK == IN_PAD


def _in_proj_body(src_ref, h_ref, wt_hbm, o_ref, stage, wb, sem, *, layer, chunks_per_step):
    j = pl.program_id(0)
    i = pl.program_id(1)
    slot = lax.rem(j, 2)

    def chunk_copy(tile, c, buf):
        row = pl.multiple_of(src_ref[tile * IN_CHUNKS_PER_TILE + c], 8)
        return pltpu.make_async_copy(wt_hbm.at[layer, pl.ds(row, IN_CHUNK), :], stage.at[buf], sem.at[buf])

    def start_fetch(tile, chunks):
        chunk_copy(tile, chunks[0], 0).start()

    def finish_fetch(tile, dst_slot, chunks):
        for n, c in enumerate(chunks):
            if n + 1 < len(chunks):
                chunk_copy(tile, chunks[n + 1], (n + 1) % 2).start()
            chunk_copy(tile, c, n % 2).wait()
            start = c * IN_CHUNK if isinstance(c, int) else pl.multiple_of(c * IN_CHUNK, IN_CHUNK)
            wb[dst_slot, pl.ds(start, IN_CHUNK), :] = stage[n % 2].astype(BF16)

    @pl.when(jnp.logical_and(j == 0, i == 0))
    def _():
        first_tile = list(range(IN_CHUNKS_PER_TILE))
        start_fetch(0, first_tile)
        finish_fetch(0, 0, first_tile)

    prefetch = jnp.logical_and(j + 1 < pl.num_programs(0), i * chunks_per_step < IN_CHUNKS_PER_TILE)
    share = [i * chunks_per_step + n for n in range(chunks_per_step)]

    @pl.when(prefetch)
    def _():
        start_fetch(j + 1, share)

    o_ref[...] = lax.dot_general(h_ref[...], wb[slot], (((1,), (1,)), ((), ())), preferred_element_type=F32)

    @pl.when(prefetch)
    def _():
        finish_fetch(j + 1, 1 - slot, share)


def _in_proj(h, w_in_t, layer):
    S, K = h.shape
    tm = min(1024, S)
    n_row_tiles = S // tm
    chunks_per_step = -(-IN_CHUNKS_PER_TILE // n_row_tiles)
    assert IN_CHUNKS_PER_TILE % chunks_per_step == 0
    grid_spec = pltpu.PrefetchScalarGridSpec(
        num_scalar_prefetch=1,
        grid=(IN_PAD // IN_TN, n_row_tiles),
        in_specs=[pl.BlockSpec((tm, K), lambda j, i, src: (i, 0)),
                  pl.BlockSpec(memory_space=pl.ANY)],
        out_specs=pl.BlockSpec((tm, IN_TN), lambda j, i, src: (i, j)),
        scratch_shapes=[pltpu.VMEM((2, IN_CHUNK, K), F32), pltpu.VMEM((2, IN_TN, K), BF16),
                        pltpu.SemaphoreType.DMA((2,))])
    return pl.pallas_call(
        functools.partial(_in_proj_body, layer=layer, chunks_per_step=chunks_per_step),
        grid_spec=grid_spec,
        out_shape=jax.ShapeDtypeStruct((S, IN_PAD), F32),
        compiler_params=_cparams(2),
        name="in_proj",
    )(jnp.asarray(_IN_CHUNK_SRC, jnp.int32), h, w_in_t)


PROJ_ROW_PARTS = 2


def _rope(seg, cosf, sin_lo, sin_hi):
    return (seg * cosf
            + pltpu.roll(seg, LANES - MLA_ROPE // 2, 1) * sin_lo
            + pltpu.roll(seg, MLA_ROPE // 2, 1) * sin_hi)


def _qproj_body(cq_ref, g_ref, w_ref, cos_ref, slo_ref, shi_ref, o_ref, a_scr):
    @pl.when(pl.program_id(1) == 0)
    def _():
        c = cq_ref[...]
        y = c * lax.rsqrt(jnp.mean(c * c, axis=-1, keepdims=True) + EPS)
        a_scr[...] = (y * g_ref[...]).astype(BF16)

    part = a_scr.shape[0] // PROJ_ROW_PARTS
    for r in range(PROJ_ROW_PARTS):
        rows = slice(r * part, (r + 1) * part)
        res = jnp.dot(a_scr[rows, :], w_ref[...], preferred_element_type=F32) * ATTN_EXP2_SCALE
        cosf, slo, shi = cos_ref[rows, :], slo_ref[rows, :], shi_ref[rows, :]
        for hh in range(o_ref.shape[0]):
            base = hh * MLA_QK_PAD
            o_ref[hh, rows, 0:MLA_NOPE] = res[:, base:base + MLA_NOPE].astype(BF16)
            o_ref[hh, rows, MLA_NOPE:MLA_QK_PAD] = _rope(
                res[:, base + MLA_NOPE:base + MLA_QK_PAD], cosf, slo, shi).astype(BF16)


def _q_proj(proj, g, w, cosf, slo, shi):
    S = proj.shape[0]
    tm = min(1024, S)
    hpb = 4
    tn = hpb * MLA_QK_PAD
    row = lambda i, j: (i, 0)
    return pl.pallas_call(
        _qproj_body,
        grid=(S // tm, MLA_HEADS // hpb),
        in_specs=[pl.BlockSpec((tm, Q_RANK), lambda i, j: (i, COL_CQ // Q_RANK)),
                  pl.BlockSpec((1, Q_RANK), lambda i, j: (0, 0)),
                  pl.BlockSpec((Q_RANK, tn), lambda i, j: (0, j)),
                  pl.BlockSpec((tm, LANES), row),
                  pl.BlockSpec((tm, LANES), row),
                  pl.BlockSpec((tm, LANES), row)],
        out_specs=pl.BlockSpec((hpb, tm, MLA_QK_PAD), lambda i, j: (j, i, 0)),
        out_shape=jax.ShapeDtypeStruct((MLA_HEADS, S, MLA_QK_PAD), BF16),
        scratch_shapes=[pltpu.VMEM((tm, Q_RANK), BF16)],
        compiler_params=_cparams(2),
        name="mla_q_proj",
    )(proj, g.reshape(1, Q_RANK), w, cosf, slo, shi)


def _kvproj_body(ckv_ref, kr_ref, g_ref, w_ref, cos_ref, slo_ref, shi_ref, k_ref, v_ref, a_scr):
    @pl.when(pl.program_id(1) == 0)
    def _():
        c = ckv_ref[...]
        y = c * lax.rsqrt(jnp.mean(c * c, axis=-1, keepdims=True) + EPS)
        a_scr[...] = (y * g_ref[...]).astype(BF16)

    width = MLA_NOPE + MLA_V
    part = a_scr.shape[0] // PROJ_ROW_PARTS
    for r in range(PROJ_ROW_PARTS):
        rows = slice(r * part, (r + 1) * part)
        res = jnp.dot(a_scr[rows, :], w_ref[...], preferred_element_type=F32)
        k_rot = _rope(kr_ref[rows, :], cos_ref[rows, :], slo_ref[rows, :], shi_ref[rows, :]).astype(BF16)
        for hh in range(k_ref.shape[0]):
            base = hh * width
            k_ref[hh, rows, 0:MLA_NOPE] = res[:, base:base + MLA_NOPE].astype(BF16)
            k_ref[hh, rows, MLA_NOPE:MLA_QK_PAD] = k_rot
            v_ref[hh, rows, :] = res[:, base + MLA_NOPE:base + width].astype(BF16)


def _kv_proj(proj, g, w, cosf, slo, shi):
    S = proj.shape[0]
    tm = min(1024, S)
    hpb = 4
    tn = hpb * (MLA_NOPE + MLA_V)
    row = lambda i, j: (i, 0)
    return pl.pallas_call(
        _kvproj_body,
        grid=(S // tm, MLA_HEADS // hpb),
        in_specs=[pl.BlockSpec((tm, KV_RANK), lambda i, j: (i, COL_CKV // KV_RANK)),
                  pl.BlockSpec((tm, LANES), lambda i, j: (i, COL_KR // LANES)),
                  pl.BlockSpec((1, KV_RANK), lambda i, j: (0, 0)),
                  pl.BlockSpec((KV_RANK, tn), lambda i, j: (0, j)),
                  pl.BlockSpec((tm, LANES), row),
                  pl.BlockSpec((tm, LANES), row),
                  pl.BlockSpec((tm, LANES), row)],
        out_specs=[pl.BlockSpec((hpb, tm, MLA_QK_PAD), lambda i, j: (j, i, 0)),
                   pl.BlockSpec((hpb, tm, MLA_V), lambda i, j: (j, i, 0))],
        out_shape=[jax.ShapeDtypeStruct((MLA_HEADS, S, MLA_QK_PAD), BF16),
                   jax.ShapeDtypeStruct((MLA_HEADS, S, MLA_V), BF16)],
        scratch_shapes=[pltpu.VMEM((tm, KV_RANK), BF16)],
        compiler_params=_cparams(2),
        name="mla_kv_proj",
    )(proj, proj, g.reshape(1, KV_RANK), w, cosf, slo, shi)


ATTN_HEADS_PER_STEP = 2
ATTN_EXP2_SCALE = float(MLA_NOPE + MLA_ROPE) ** -0.5 * 1.4426950408889634
ATTN_STRIP = 64


def _attn_body(q_ref, k_ref, v_ref, z_ref, o_ref, s_scr, p_scr, m_scr, a_scr, acc_scr, *, blk):
    i = pl.program_id(1)
    A, B = 0, 1
    ones_col = (lax.broadcasted_iota(jnp.int32, (blk, MLA_V), 1) == 0).astype(BF16)
    col_chunk = lax.broadcasted_iota(jnp.int32, (ATTN_STRIP, blk), 1) // CHUNK
    row_in_strip = lax.broadcasted_iota(jnp.int32, (ATTN_STRIP, blk), 0)

    def score(h, kb):
        k = k_ref[h, pl.ds(pl.multiple_of(kb * blk, blk), blk), :]
        s_scr[h] = lax.dot_general(q_ref[h], k, (((1,), (1,)), ((), ())), preferred_element_type=F32)

    def softmax(h, diagonal):
        for r in range(blk // ATTN_STRIP):
            rows = slice(r * ATTN_STRIP, (r + 1) * ATTN_STRIP)
            s = s_scr[h, rows, :]
            if diagonal:
                s = jnp.where(col_chunk <= (row_in_strip + r * ATTN_STRIP) // CHUNK, s, NEG)
                m_new = jnp.broadcast_to(jnp.max(s, axis=-1, keepdims=True), (ATTN_STRIP, LANES))
                a_scr[h, rows, :] = jnp.ones((ATTN_STRIP, LANES), F32)
            else:
                m_old = m_scr[h, rows, :]
                m_new = jnp.maximum(m_old, jnp.max(s, axis=-1, keepdims=True))
                a_scr[h, rows, :] = jnp.exp2(m_old - m_new)
            m_scr[h, rows, :] = m_new
            p_scr[h, rows, :] = jnp.exp2(s - jnp.concatenate([m_new] * (blk // LANES), axis=1)).astype(BF16)

    def apply_values(h, kb):
        v = v_ref[h, pl.ds(pl.multiple_of(kb * blk, blk), blk), :]
        pv = jnp.dot(p_scr[h], jnp.concatenate([v, ones_col], axis=1), preferred_element_type=F32)
        a = a_scr[h]
        acc_scr[h] = jnp.concatenate([a, a], axis=1) * acc_scr[h] + pv

    acc_scr[...] = jnp.zeros_like(acc_scr)
    score(A, i)
    softmax(A, True)
    score(B, i)
    softmax(B, True)
    score(A, 0)
    apply_values(A, i)

    def step(j):
        softmax(A, False)
        score(B, j - 1)
        apply_values(B, jnp.where(j == 1, i, j - 2))
        softmax(B, False)
        score(A, j)
        apply_values(A, j - 1)

    def two_steps(t, carry):
        step(2 * t + 1)
        step(2 * t + 2)
        return carry

    lax.fori_loop(0, lax.shift_right_logical(i, 1), two_steps, 0)

    @pl.when(lax.bitwise_and(i, 1) == 1)
    def _():
        step(i)

    apply_values(B, jnp.maximum(i - 1, 0))
    for h in (A, B):
        cols = slice(h * MLA_V, (h + 1) * MLA_V)
        attn = acc_scr[h, :, 0:MLA_V] / acc_scr[h, :, MLA_V:MLA_V + 1]
        o_ref[:, cols] = (attn * _silu(z_ref[:, cols])).astype(o_ref.dtype)


def _attention(q, k, v, proj):
    H, S, _ = q.shape
    blk = min(512, S)
    hp = ATTN_HEADS_PER_STEP
    assert hp == 2
    return pl.pallas_call(
        functools.partial(_attn_body, blk=blk),
        grid=(H // hp, S // blk),
        in_specs=[pl.BlockSpec((hp, blk, MLA_QK_PAD), lambda h, i: (h, i, 0)),
                  pl.BlockSpec((hp, S, MLA_QK_PAD), lambda h, i: (h, 0, 0)),
                  pl.BlockSpec((hp, S, MLA_V), lambda h, i: (h, 0, 0)),
                  pl.BlockSpec((blk, hp * MLA_V), lambda h, i: (i, COL_ZA // (hp * MLA_V) + h))],
        out_specs=pl.BlockSpec((blk, hp * MLA_V), lambda h, i: (i, h)),
        out_shape=jax.ShapeDtypeStruct((S, MLA_WIDTH), BF16),
        scratch_shapes=[pltpu.VMEM((hp, blk, blk), F32), pltpu.VMEM((hp, blk, blk), BF16),
                        pltpu.VMEM((hp, blk, LANES), F32), pltpu.VMEM((hp, blk, LANES), F32),
                        pltpu.VMEM((hp, blk, 2 * MLA_V), F32)],
        compiler_params=_cparams(2),
        name="mla_attention",
    )(q, k, v, proj)


def _sgu_body(u_ref, v_ref, z_ref, lng_ref, lnb_ref, ws_ref, bst_ref, o_ref):
    v = v_ref[...]
    mu = jnp.mean(v, axis=-1, keepdims=True)
    vc = v - mu
    var = jnp.mean(vc * vc, axis=-1, keepdims=True)
    vn = (vc * lax.rsqrt(var + EPS) * lng_ref[...] + lnb_ref[...]).astype(BF16)
    out_chunk = lax.broadcasted_iota(jnp.int32, (SG_BLOCK, SG_BLOCK), 0) // CHUNK
    in_chunk = lax.broadcasted_iota(jnp.int32, (SG_BLOCK, SG_BLOCK), 1) // CHUNK
    causal = in_chunk <= out_chunk
    bst = bst_ref[...]
    for g in range(SG_GROUPS):
        w = jnp.where(causal, ws_ref[g], 0.0).astype(BF16)
        cols = slice(g * SG_GROUP_DIM, (g + 1) * SG_GROUP_DIM)
        bias = bst[:, g:g + 1]
        for b in range(u_ref.shape[0] // SG_BLOCK):
            rows = slice(b * SG_BLOCK, (b + 1) * SG_BLOCK)
            s = jnp.dot(w, vn[rows, cols], preferred_element_type=F32) + bias
            o_ref[rows, cols] = (u_ref[rows, cols] * s * _silu(z_ref[rows, cols])).astype(o_ref.dtype)


def _spatial_gating(proj, ln_g, ln_b, w_s, b_s):
    S = proj.shape[0]
    tm = min(512, S)
    blk = lambda c: pl.BlockSpec((tm, SG_WIDTH), lambda i: (i, c // SG_WIDTH))
    full = lambda shape: pl.BlockSpec(shape, lambda i: (0,) * len(shape))
    return pl.pallas_call(
        _sgu_body,
        grid=(S // tm,),
        in_specs=[blk(COL_SGU), blk(COL_SGV), blk(COL_ZB),
                  full((1, SG_WIDTH)), full((1, SG_WIDTH)),
                  full((SG_GROUPS, SG_BLOCK, SG_BLOCK)), full((SG_BLOCK, SG_GROUPS))],
        out_specs=pl.BlockSpec((tm, SG_WIDTH), lambda i: (i, 0)),
        out_shape=jax.ShapeDtypeStruct((S, SG_WIDTH), BF16),
        compiler_params=_cparams(1),
        name="spatial_gating",
    )(proj, proj, proj, ln_g.reshape(1, SG_WIDTH), ln_b.reshape(1, SG_WIDTH), w_s, jnp.transpose(b_s))


def _mlstm_body(x_ref, v_ref, og_ref, z_ref, if_ref, cw_ref, cb_ref, wq_ref, wk_ref,
                bi_ref, bf_ref, gn_ref, o_ref, xs_scr, ct_scr, n_scr, m_scr, h_scr, *, T, L):
    halo = 8

    @pl.when(pl.program_id(0) == 0)
    def _():
        xs_scr[0:halo, :] = jnp.zeros((halo, ML_WIDTH), F32)
        ct_scr[...] = jnp.zeros_like(ct_scr)
        n_scr[...] = jnp.zeros_like(n_scr)
        m_scr[...] = jnp.zeros_like(m_scr)

    xs_scr[halo:halo + T, :] = x_ref[...]
    cw = cw_ref[...]
    xc = cb_ref[...]
    for kk in range(CONV_K):
        off = halo - (CONV_K - 1) + kk
        xc = xc + cw[kk:kk + 1, :] * xs_scr[off:off + T, :]
    xs_scr[0:halo, :] = xs_scr[T:T + halo, :]
    xcb = _silu(xc).astype(BF16)

    gates = if_ref[...]
    head_lane = lax.broadcasted_iota(jnp.int32, gates.shape, 1) < ML_HEADS
    ig = jnp.where(head_lane, gates + bi_ref[...], 0.0)
    fg = jnp.where(head_lane, pltpu.roll(gates, LANES - ML_HEADS, 1) + bf_ref[...], 0.0)
    lf = jnp.minimum(fg, 0.0) - jnp.log1p(jnp.exp(-jnp.abs(fg)))
    rr = lax.broadcasted_iota(jnp.int32, (L, L), 0)
    cc = lax.broadcasted_iota(jnp.int32, (L, L), 1)
    tril = cc <= rr
    tri_f = tril.astype(F32)
    nc = T // L
    g_l, G_l, a_l, e_l, r_l = [], [], [], [], []
    for c in range(nc):
        rows = slice(c * L, (c + 1) * L)
        g_c = jnp.dot(tri_f, lf[rows], precision=lax.Precision.HIGHEST, preferred_element_type=F32)
        G_c = g_c[L - 1:L, :]
        w_end = G_c - g_c + ig[rows]
        a_c = jnp.max(w_end, axis=0, keepdims=True)
        g_l.append(g_c)
        G_l.append(G_c)
        a_l.append(a_c)
        e_l.append(jnp.exp(w_end - a_c))
        r_l.append(ig[rows] - g_c)
    r_t = jnp.transpose(jnp.concatenate(r_l, axis=0))

    q_l, k_l = [], []
    for h in range(ML_HEADS):
        cols = slice(h * ML_HEAD_DIM, (h + 1) * ML_HEAD_DIM)
        q_l.append(jnp.dot(xcb[:, cols], wq_ref[h], preferred_element_type=F32))
        k_l.append(jnp.dot(xcb[:, cols], wk_ref[h], preferred_element_type=F32) * (ML_HEAD_DIM ** -0.5))

    m_vec = m_scr[0:1, :]
    for c in range(nc):
        rows = slice(c * L, (c + 1) * L)
        g_c, G_c, a_c, e_c = g_l[c], G_l[c], a_l[c], e_l[c]
        m_new = jnp.maximum(G_c + m_vec, a_c)
        sp = jnp.exp(G_c + m_vec - m_new)
        sl = jnp.exp(a_c - m_new)
        for h in range(ML_HEADS):
            cols = slice(h * ML_HEAD_DIM, (h + 1) * ML_HEAD_DIM)
            q_c = q_l[h][rows]
            k_c = k_l[h][rows]
            v_c = v_ref[rows, cols]
            q_cb = q_c.astype(BF16)
            g_col = g_c[:, h:h + 1]
            d = jnp.where(tril, g_col + r_t[h:h + 1, rows], NEG)
            inter_log = g_col + m_vec[:, h:h + 1]
            m_i = jnp.maximum(jnp.max(d, axis=1, keepdims=True), inter_log)
            p = jnp.exp(d - m_i)
            qk = lax.dot_general(q_cb, k_c.astype(BF16), (((1,), (1,)), ((), ())),
                                 preferred_element_type=F32)
            sm = qk * p
            inter_scale = jnp.exp(inter_log - m_i)
            ct = ct_scr[h]
            n_prev = n_scr[h:h + 1, :]
            num = (jnp.dot(sm.astype(BF16), v_c.astype(BF16), preferred_element_type=F32)
                   + inter_scale * jnp.dot(q_cb, ct.astype(BF16), preferred_element_type=F32))
            den = (jnp.sum(sm, axis=1, keepdims=True)
                   + inter_scale * jnp.sum(q_c * n_prev, axis=1, keepdims=True))
            h_scr[rows, cols] = num / jnp.maximum(jnp.abs(den), jnp.exp(-m_i))

            e_col = e_c[:, h:h + 1]
            ev = (e_col * v_c).astype(BF16)
            c_loc_t = jnp.dot(jnp.transpose(k_c).astype(BF16), ev, preferred_element_type=F32)
            sp_h = sp[:, h:h + 1]
            sl_h = sl[:, h:h + 1]
            ct_scr[h] = sp_h * ct + sl_h * c_loc_t
            n_scr[h:h + 1, :] = sp_h * n_prev + sl_h * jnp.sum(e_col * k_c, axis=0, keepdims=True)
        m_vec = m_new
    m_scr[0:1, :] = m_vec

    for h in range(ML_HEADS):
        cols = slice(h * ML_HEAD_DIM, (h + 1) * ML_HEAD_DIM)
        hs = h_scr[:, cols] * _sigmoid(og_ref[:, cols])
        mu = jnp.mean(hs, axis=-1, keepdims=True)
        hc = hs - mu
        var = jnp.mean(hc * hc, axis=-1, keepdims=True)
        y = hc * lax.rsqrt(var + EPS) * gn_ref[:, cols]
        o_ref[:, cols] = (y * _silu(z_ref[:, cols])).astype(o_ref.dtype)


def _mlstm(proj, conv_w, conv_b, w_q, w_k, b_i, b_f, gn_g):
    S = proj.shape[0]
    T = min(512, S)
    L = ML_CHUNK
    wide = lambda c: pl.BlockSpec((T, ML_WIDTH), lambda i: (i, c // ML_WIDTH))
    gate = lambda c: pl.BlockSpec((T, LANES), lambda i: (i, c // LANES))
    full = lambda shape: pl.BlockSpec(shape, lambda i: (0,) * len(shape))
    pad_gate_bias = lambda b: jnp.pad(b, (0, LANES - ML_HEADS)).reshape(1, LANES)
    return pl.pallas_call(
        functools.partial(_mlstm_body, T=T, L=L),
        grid=(S // T,),
        in_specs=[wide(COL_MLX), wide(COL_MLV), wide(COL_MLO), wide(COL_ZC),
                  gate(COL_MLI),
                  full((CONV_K, ML_WIDTH)), full((1, ML_WIDTH)),
                  full((ML_HEADS, ML_HEAD_DIM, ML_HEAD_DIM)), full((ML_HEADS, ML_HEAD_DIM, ML_HEAD_DIM)),
                  full((1, LANES)), full((1, LANES)), full((1, ML_WIDTH))],
        out_specs=pl.BlockSpec((T, ML_WIDTH), lambda i: (i, 0)),
        out_shape=jax.ShapeDtypeStruct((S, ML_WIDTH), BF16),
        scratch_shapes=[pltpu.VMEM((T + 8, ML_WIDTH), F32),
                        pltpu.VMEM((ML_HEADS, ML_HEAD_DIM, ML_HEAD_DIM), F32),
                        pltpu.VMEM((8, ML_HEAD_DIM), F32),
                        pltpu.VMEM((8, LANES), F32),
                        pltpu.VMEM((T, ML_WIDTH), F32)],
        compiler_params=_cparams(1),
        name="mlstm",
    )(proj, proj, proj, proj, proj, conv_w, conv_b.reshape(1, ML_WIDTH), w_q, w_k,
      pad_gate_bias(b_i), pad_gate_bias(b_f), gn_g.reshape(1, ML_WIDTH))


MERGE_TN = 1024
MERGE_CHUNK_ROWS = 512
MERGE_K = MLA_WIDTH + SG_WIDTH + ML_WIDTH
MERGE_CHUNKS = MERGE_K // MERGE_CHUNK_ROWS


def _merge_body(ya_ref, yb_ref, yc_ref, wa_hbm, wb_hbm, wc_hbm, g0_ref, g1_ref, g2_ref, o_ref,
                stage, wt, sem, *, layer, chunks_per_step):
    j = pl.program_id(0)
    i = pl.program_id(1)
    slot = lax.rem(j, 2)
    sources = ((wa_hbm, MLA_WIDTH), (wb_hbm, SG_WIDTH), (wc_hbm, ML_WIDTH))

    def start_chunk(tile, c, buf):
        cols = pl.ds(pl.multiple_of(tile * MERGE_TN, MERGE_TN), MERGE_TN)

        def issue(chunk):
            first = 0
            for w_hbm, rows in sources:
                if chunk * MERGE_CHUNK_ROWS < first + rows:
                    src = w_hbm.at[layer, pl.ds(chunk * MERGE_CHUNK_ROWS - first, MERGE_CHUNK_ROWS), cols]
                    pltpu.make_async_copy(src, stage.at[buf], sem.at[buf]).start()
                    return
                first += rows

        if isinstance(c, int):
            issue(c)
        else:
            for chunk in range(MERGE_CHUNKS):
                pl.when(c == chunk)(functools.partial(issue, chunk))

    def wait_chunk(buf):
        src = wa_hbm.at[layer, pl.ds(0, MERGE_CHUNK_ROWS), pl.ds(0, MERGE_TN)]
        pltpu.make_async_copy(src, stage.at[buf], sem.at[buf]).wait()

    def start_fetch(tile, chunks):
        start_chunk(tile, chunks[0], 0)

    def finish_fetch(tile, dst_slot, chunks):
        for n, c in enumerate(chunks):
            if n + 1 < len(chunks):
                start_chunk(tile, chunks[n + 1], (n + 1) % 2)
            wait_chunk(n % 2)
            start = (c * MERGE_CHUNK_ROWS if isinstance(c, int)
                     else pl.multiple_of(c * MERGE_CHUNK_ROWS, MERGE_CHUNK_ROWS))
            wt[dst_slot, pl.ds(start, MERGE_CHUNK_ROWS), :] = stage[n % 2].astype(BF16)

    @pl.when(jnp.logical_and(j == 0, i == 0))
    def _():
        first_tile = list(range(MERGE_CHUNKS))
        start_fetch(0, first_tile)
        finish_fetch(0, 0, first_tile)

    prefetch = jnp.logical_and(j + 1 < pl.num_programs(0), i * chunks_per_step < MERGE_CHUNKS)
    share = [i * chunks_per_step + n for n in range(chunks_per_step)]

    @pl.when(prefetch)
    def _():
        start_fetch(j + 1, share)

    w = wt.at[slot]
    pa = jnp.dot(ya_ref[...], w[0:MLA_WIDTH, :], preferred_element_type=F32)
    pb = jnp.dot(yb_ref[...], w[MLA_WIDTH:MLA_WIDTH + SG_WIDTH, :], preferred_element_type=F32)
    pc = jnp.dot(yc_ref[...], w[MLA_WIDTH + SG_WIDTH:MERGE_K, :], preferred_element_type=F32)
    merged = _sigmoid(g0_ref[...]) * pa + _sigmoid(g1_ref[...]) * pb + _sigmoid(g2_ref[...]) * pc
    o_ref[...] = merged.astype(o_ref.dtype)

    @pl.when(prefetch)
    def _():
        finish_fetch(j + 1, 1 - slot, share)


def _merge(y_a, y_b, y_c, w_pa, w_pb, w_pc, layer, proj):
    S = y_a.shape[0]
    D = w_pa.shape[-1]
    tm = min(512, S)
    tn = MERGE_TN
    n_row_tiles = S // tm
    chunks_per_step = -(-MERGE_CHUNKS // n_row_tiles)
    assert MERGE_CHUNKS % chunks_per_step == 0
    act = lambda width: pl.BlockSpec((tm, width), lambda j, i: (i, 0))
    hbm = pl.BlockSpec(memory_space=pl.ANY)
    gate = lambda b: pl.BlockSpec((tm, tn), lambda j, i: (i, (COL_GATE + b * D) // tn + j))
    return pl.pallas_call(
        functools.partial(_merge_body, layer=layer, chunks_per_step=chunks_per_step),
        grid=(D // tn, n_row_tiles),
        in_specs=[act(MLA_WIDTH), act(SG_WIDTH), act(ML_WIDTH), hbm, hbm, hbm,
                  gate(0), gate(1), gate(2)],
        out_specs=pl.BlockSpec((tm, tn), lambda j, i: (i, j)),
        out_shape=jax.ShapeDtypeStruct((S, D), BF16),
        scratch_shapes=[pltpu.VMEM((2, MERGE_CHUNK_ROWS, tn), F32), pltpu.VMEM((2, MERGE_K, tn), BF16),
                        pltpu.SemaphoreType.DMA((2,))],
        compiler_params=_cparams(2),
        name="gated_merge",
    )(y_a, y_b, y_c, w_pa, w_pb, w_pc, proj, proj, proj)


OUT_TN = 1024
OUT_CHUNKS = 8


def _outproj_body(m_ref, w_hbm, x_ref, o_ref, stage, wb, sem, *, layer, chunks_per_step):
    j = pl.program_id(0)
    i = pl.program_id(1)
    slot = lax.rem(j, 2)
    chunk_rows = stage.shape[1]

    def chunk_copy(tile, c, buf):
        rows = pl.ds(c * chunk_rows if isinstance(c, int) else pl.multiple_of(c * chunk_rows, chunk_rows), chunk_rows)
        cols = pl.ds(pl.multiple_of(tile * OUT_TN, OUT_TN), OUT_TN)
        return pltpu.make_async_copy(w_hbm.at[layer, rows, cols], stage.at[buf], sem.at[buf])

    def start_fetch(tile, chunks):
        chunk_copy(tile, chunks[0], 0).start()

    def finish_fetch(tile, dst_slot, chunks):
        for n, c in enumerate(chunks):
            if n + 1 < len(chunks):
                chunk_copy(tile, chunks[n + 1], (n + 1) % 2).start()
            chunk_copy(tile, c, n % 2).wait()
            start = c * chunk_rows if isinstance(c, int) else pl.multiple_of(c * chunk_rows, chunk_rows)
            wb[dst_slot, pl.ds(start, chunk_rows), :] = stage[n % 2].astype(BF16)

    @pl.when(jnp.logical_and(j == 0, i == 0))
    def _():
        first_tile = list(range(OUT_CHUNKS))
        start_fetch(0, first_tile)
        finish_fetch(0, 0, first_tile)

    prefetch = jnp.logical_and(j + 1 < pl.num_programs(0), i * chunks_per_step < OUT_CHUNKS)
    share = [i * chunks_per_step + n for n in range(chunks_per_step)]

    @pl.when(prefetch)
    def _():
        start_fetch(j + 1, share)

    o_ref[...] = x_ref[...] + jnp.dot(m_ref[...], wb[slot], preferred_element_type=F32)

    @pl.when(prefetch)
    def _():
        finish_fetch(j + 1, 1 - slot, share)


def _out_proj(merged, w_out, layer, x):
    S, D = x.shape
    tm = min(512, S)
    n_row_tiles = S // tm
    chunks_per_step = -(-OUT_CHUNKS // n_row_tiles)
    assert OUT_CHUNKS % chunks_per_step == 0
    return pl.pallas_call(
        functools.partial(_outproj_body, layer=layer, chunks_per_step=chunks_per_step),
        grid=(D // OUT_TN, n_row_tiles),
        in_specs=[pl.BlockSpec((tm, D), lambda j, i: (i, 0)),
                  pl.BlockSpec(memory_space=pl.ANY),
                  pl.BlockSpec((tm, OUT_TN), lambda j, i: (i, j))],
        out_specs=pl.BlockSpec((tm, OUT_TN), lambda j, i: (i, j)),
        out_shape=jax.ShapeDtypeStruct((S, D), F32),
        scratch_shapes=[pltpu.VMEM((2, D // OUT_CHUNKS, OUT_TN), F32), pltpu.VMEM((2, D, OUT_TN), BF16),
                        pltpu.SemaphoreType.DMA((2,))],
        compiler_params=_cparams(2),
        name="out_proj",
    )(merged, w_out, x)


def _pack_w_uq(w_uq):
    lead = w_uq.shape[:-1]
    w = w_uq.reshape(lead + (MLA_HEADS, MLA_NOPE + MLA_ROPE)).astype(BF16)
    w = jnp.pad(w, [(0, 0)] * len(lead) + [(0, 0), (0, MLA_QK_PAD - MLA_NOPE - MLA_ROPE)])
    return w.reshape(lead + (MLA_HEADS * MLA_QK_PAD,))


def _rope_tables(S):
    half = MLA_ROPE // 2
    inv_freq = ROPE_THETA ** (-jnp.arange(0, MLA_ROPE, 2, dtype=F32) / MLA_ROPE)
    ang = jnp.arange(S, dtype=F32)[:, None] * inv_freq[None, :]
    cos, sin = jnp.cos(ang), jnp.sin(ang)
    z = lambda n: jnp.zeros((S, n), F32)
    cosf = jnp.concatenate([cos, cos, z(LANES - MLA_ROPE)], axis=1)
    sin_lo = jnp.concatenate([-sin, z(LANES - half)], axis=1)
    sin_hi = jnp.concatenate([z(half), sin, z(LANES - MLA_ROPE)], axis=1)
    return cosf, sin_lo, sin_hi


def _layer(x, tables, layer, norm_g, w_in_t, mla_gq, mla_gkv, w_uq_p, w_ukv_b, sg_ln_g, sg_ln_b, sg_ws, sg_bs,
           ml_conv_w, ml_conv_b, ml_wq_b, ml_wk_b, ml_bi, ml_bf, ml_gn_g, w_pa, w_pb, w_pc, w_out):
    h = _rmsnorm(x, norm_g, BF16)
    proj = _in_proj(h, w_in_t, layer)
    q = _q_proj(proj, mla_gq, w_uq_p, *tables)
    k, v = _kv_proj(proj, mla_gkv, w_ukv_b, *tables)
    y_a = _attention(q, k, v, proj)
    y_b = _spatial_gating(proj, sg_ln_g, sg_ln_b, sg_ws, sg_bs)
    y_c = _mlstm(proj, ml_conv_w, ml_conv_b, ml_wq_b, ml_wk_b, ml_bi, ml_bf, ml_gn_g)
    merged = _merge(y_a, y_b, y_c, w_pa, w_pb, w_pc, layer, proj)
    return _out_proj(merged, w_out, layer, x)


def kernel(x, norm_g, w_in, mla_gq, mla_gkv, mla_wuq, mla_wukv, sg_ln_g, sg_ln_b, sg_ws, sg_bs, ml_conv_w, ml_conv_b, ml_wq, ml_wk, ml_bi, ml_bf, ml_gn_g, w_pa, w_pb, w_pc, w_out, final_g):
    B, S, D = x.shape
    depth = w_in.shape[0]
    tables = _rope_tables(S)
    cast = lambda w: w.astype(BF16)
    w_in_t = jnp.swapaxes(w_in, 1, 2)
    outs = []
    for b in range(B):
        xb = x[b]
        for l in range(depth):
            xb = _layer(xb, tables, l, norm_g[l], w_in_t, mla_gq[l], mla_gkv[l],
                        _pack_w_uq(mla_wuq[l]), cast(mla_wukv[l]),
                        sg_ln_g[l], sg_ln_b[l], sg_ws[l], sg_bs[l], ml_conv_w[l], ml_conv_b[l],
                        cast(ml_wq[l]), cast(ml_wk[l]), ml_bi[l], ml_bf[l], ml_gn_g[l],
                        w_pa, w_pb, w_pc, w_out)
        outs.append(_rmsnorm(xb, final_g, x.dtype))
    return outs[0][None] if B == 1 else jnp.stack(outs, axis=0)
```

```python
import functools

import jax
import jax.numpy as jnp
from jax import lax
from jax.experimental import pallas as pl
from jax.experimental.pallas import tpu as pltpu

F32 = jnp.float32
BF16 = jnp.bfloat16

D_MODEL = 4096
CHUNK = 64
EPS = 1e-6

MLA_HEADS = 16
MLA_NOPE = 128
MLA_ROPE = 64
MLA_V = 128
Q_RANK = 1024
KV_RANK = 512
ROPE_THETA = 10000.0
MLA_WIDTH = MLA_HEADS * MLA_V
MLA_QK_PAD = 256

SG_BLOCK = 128
SG_GROUPS = 4
SG_WIDTH = 1024
SG_GROUP_DIM = SG_WIDTH // SG_GROUPS

ML_HEADS = 4
ML_WIDTH = 1024
ML_HEAD_DIM = 256
CONV_K = 4
ML_CHUNK = 128

LANES = 128
NEG = -1e30

COL_CQ = 0
COL_ZA = 1024
COL_SGU = 3072
COL_SGV = 4096
COL_ZB = 5120
COL_MLX = 6144
COL_MLV = 7168
COL_MLO = 8192
COL_ZC = 9216
COL_GATE = 10240
COL_CKV = 22528
COL_KR = 23040
COL_MLI = 23168
IN_PAD = 23552

VMEM_LIMIT = 56 * 1024 * 1024


def _cparams(n_axes, vmem=VMEM_LIMIT):
    return pltpu.CompilerParams(dimension_semantics=("arbitrary",) * n_axes,
                                vmem_limit_bytes=vmem)


def _sigmoid(x):
    return 1.0 / (1.0 + jnp.exp(-x))


def _silu(x):
    return x * _sigmoid(x)


def _rmsnorm_body(x_ref, g_ref, o_ref):
    x = x_ref[...]
    y = x * lax.rsqrt(jnp.mean(x * x, axis=-1, keepdims=True) + EPS)
    o_ref[...] = (y * g_ref[...]).astype(o_ref.dtype)


def _rmsnorm(x, g, out_dtype):
    S, D = x.shape
    tm = min(512, S)
    return pl.pallas_call(
        _rmsnorm_body,
        grid=(S // tm,),
        in_specs=[pl.BlockSpec((tm, D), lambda i: (i, 0)),
                  pl.BlockSpec((1, D), lambda i: (0, 0))],
        out_specs=pl.BlockSpec((tm, D), lambda i: (i, 0)),
        out_shape=jax.ShapeDtypeStruct((S, D), out_dtype),
        compiler_params=_cparams(1),
        name="rmsnorm",
    )(x, g.reshape(1, D))


IN_TN = 1024
IN_CHUNK = 128
IN_CHUNKS_PER_TILE = IN_TN // IN_CHUNK

_IN_TILE_SRC = (0, 1600, 2624, 3648, 4672, 5696, 6720, 7744, 8768, 9792) + tuple(
    10824 + IN_TN * t for t in range(3 * D_MODEL // IN_TN))
_IN_CHUNK_SRC = tuple(src + IN_CHUNK * c for src in _IN_TILE_SRC for c in range(IN_CHUNKS_PER_TILE)) + (
    1024, 1152, 1280, 1408, 1536, 10816, 0, 0)
assert len(_IN_CHUNK_SRC) * IN_CHUNK == IN_PAD


def _in_proj_body(src_ref, h_ref, wt_hbm, o_ref, stage, wb, sem, *, layer, chunks_per_step):
    j = pl.program_id(0)
    i = pl.program_id(1)
    slot = lax.rem(j, 2)

    def chunk_copy(tile, c, buf):
        row = pl.multiple_of(src_ref[tile * IN_CHUNKS_PER_TILE + c], 8)
        return pltpu.make_async_copy(wt_hbm.at[layer, pl.ds(row, IN_CHUNK), :], stage.at[buf], sem.at[buf])

    def start_fetch(tile, chunks):
        chunk_copy(tile, chunks[0], 0).start()

    def finish_fetch(tile, dst_slot, chunks):
        for n, c in enumerate(chunks):
            if n + 1 < len(chunks):
                chunk_copy(tile, chunks[n + 1], (n + 1) % 2).start()
            chunk_copy(tile, c, n % 2).wait()
            start = c * IN_CHUNK if isinstance(c, int) else pl.multiple_of(c * IN_CHUNK, IN_CHUNK)
            wb[dst_slot, pl.ds(start, IN_CHUNK), :] = stage[n % 2].astype(BF16)

    @pl.when(jnp.logical_and(j == 0, i == 0))
    def _():
        first_tile = list(range(IN_CHUNKS_PER_TILE))
        start_fetch(0, first_tile)
        finish_fetch(0, 0, first_tile)

    prefetch = jnp.logical_and(j + 1 < pl.num_programs(0), i * chunks_per_step < IN_CHUNKS_PER_TILE)
    share = [i * chunks_per_step + n for n in range(chunks_per_step)]

    @pl.when(prefetch)
    def _():
        start_fetch(j + 1, share)

    o_ref[...] = lax.dot_general(h_ref[...], wb[slot], (((1,), (1,)), ((), ())), preferred_element_type=F32)

    @pl.when(prefetch)
    def _():
        finish_fetch(j + 1, 1 - slot, share)


def _in_proj(h, w_in_t, layer):
    S, K = h.shape
    tm = min(1024, S)
    n_row_tiles = S // tm
    chunks_per_step = -(-IN_CHUNKS_PER_TILE // n_row_tiles)
    assert IN_CHUNKS_PER_TILE % chunks_per_step == 0
    grid_spec = pltpu.PrefetchScalarGridSpec(
        num_scalar_prefetch=1,
        grid=(IN_PAD // IN_TN, n_row_tiles),
        in_specs=[pl.BlockSpec((tm, K), lambda j, i, src: (i, 0)),
                  pl.BlockSpec(memory_space=pl.ANY)],
        out_specs=pl.BlockSpec((tm, IN_TN), lambda j, i, src: (i, j)),
        scratch_shapes=[pltpu.VMEM((2, IN_CHUNK, K), F32), pltpu.VMEM((2, IN_TN, K), BF16),
                        pltpu.SemaphoreType.DMA((2,))])
    return pl.pallas_call(
        functools.partial(_in_proj_body, layer=layer, chunks_per_step=chunks_per_step),
        grid_spec=grid_spec,
        out_shape=jax.ShapeDtypeStruct((S, IN_PAD), F32),
        compiler_params=_cparams(2),
        name="in_proj",
    )(jnp.asarray(_IN_CHUNK_SRC, jnp.int32), h, w_in_t)


PROJ_ROW_PARTS = 2


def _rope(seg, cosf, sin_lo, sin_hi):
    return (seg * cosf
            + pltpu.roll(seg, LANES - MLA_ROPE // 2, 1) * sin_lo
            + pltpu.roll(seg, MLA_ROPE // 2, 1) * sin_hi)


def _qproj_body(cq_ref, g_ref, wn_ref, wr_ref, cos_ref, slo_ref, shi_ref, o_ref, a_scr):
    @pl.when(pl.program_id(1) == 0)
    def _():
        c = cq_ref[...]
        y = c * lax.rsqrt(jnp.mean(c * c, axis=-1, keepdims=True) + EPS)
        a_scr[...] = (y * g_ref[...]).astype(BF16)

    part = a_scr.shape[0] // PROJ_ROW_PARTS
    for r in range(PROJ_ROW_PARTS):
        rows = slice(r * part, (r + 1) * part)
        nope = jnp.dot(a_scr[rows, :], wn_ref[...], preferred_element_type=F32) * ATTN_EXP2_SCALE
        rope = jnp.dot(a_scr[rows, :], wr_ref[...], preferred_element_type=F32) * ATTN_EXP2_SCALE
        cosf, slo, shi = cos_ref[rows, :], slo_ref[rows, :], shi_ref[rows, :]
        for hh in range(o_ref.shape[0]):
            o_ref[hh, rows, 0:MLA_NOPE] = nope[:, hh * MLA_NOPE:(hh + 1) * MLA_NOPE].astype(BF16)
            pair = rope[:, (hh // 2) * LANES:(hh // 2 + 1) * LANES]
            seg = pair if hh % 2 == 0 else pltpu.roll(pair, MLA_ROPE, 1)
            o_ref[hh, rows, MLA_NOPE:MLA_QK_PAD] = _rope(seg, cosf, slo, shi).astype(BF16)


def _q_proj(proj, g, w, cosf, slo, shi):
    S = proj.shape[0]
    tm = min(1024, S)
    hpb = 8
    rope0 = MLA_HEADS * MLA_NOPE // (hpb * MLA_ROPE)
    row = lambda i, j: (i, 0)
    return pl.pallas_call(
        _qproj_body,
        grid=(S // tm, MLA_HEADS // hpb),
        in_specs=[pl.BlockSpec((tm, Q_RANK), lambda i, j: (i, COL_CQ // Q_RANK)),
                  pl.BlockSpec((1, Q_RANK), lambda i, j: (0, 0)),
                  pl.BlockSpec((Q_RANK, hpb * MLA_NOPE), lambda i, j: (0, j)),
                  pl.BlockSpec((Q_RANK, hpb * MLA_ROPE), lambda i, j: (0, rope0 + j)),
                  pl.BlockSpec((tm, LANES), row),
                  pl.BlockSpec((tm, LANES), row),
                  pl.BlockSpec((tm, LANES), row)],
        out_specs=pl.BlockSpec((hpb, tm, MLA_QK_PAD), lambda i, j: (j, i, 0)),
        out_shape=jax.ShapeDtypeStruct((MLA_HEADS, S, MLA_QK_PAD), BF16),
        scratch_shapes=[pltpu.VMEM((tm, Q_RANK), BF16)],
        compiler_params=_cparams(2),
        name="mla_q_proj",
    )(proj, g.reshape(1, Q_RANK), w, w, cosf, slo, shi)


def _kvproj_body(ckv_ref, kr_ref, g_ref, w_ref, cos_ref, slo_ref, shi_ref, k_ref, v_ref, a_scr):
    @pl.when(pl.program_id(1) == 0)
    def _():
        c = ckv_ref[...]
        y = c * lax.rsqrt(jnp.mean(c * c, axis=-1, keepdims=True) + EPS)
        a_scr[...] = (y * g_ref[...]).astype(BF16)

    width = MLA_NOPE + MLA_V
    part = a_scr.shape[0] // PROJ_ROW_PARTS
    for r in range(PROJ_ROW_PARTS):
        rows = slice(r * part, (r + 1) * part)
        res = jnp.dot(a_scr[rows, :], w_ref[...], preferred_element_type=F32)
        k_rot = _rope(kr_ref[rows, :], cos_ref[rows, :], slo_ref[rows, :], shi_ref[rows, :]).astype(BF16)
        for hh in range(k_ref.shape[0]):
            base = hh * width
            k_ref[hh, rows, 0:MLA_NOPE] = res[:, base:base + MLA_NOPE].astype(BF16)
            k_ref[hh, rows, MLA_NOPE:MLA_QK_PAD] = k_rot
            v_ref[hh, rows, :] = res[:, base + MLA_NOPE:base + width].astype(BF16)


def _kv_proj(proj, g, w, cosf, slo, shi):
    S = proj.shape[0]
    tm = min(1024, S)
    hpb = 8
    tn = hpb * (MLA_NOPE + MLA_V)
    row = lambda i, j: (i, 0)
    return pl.pallas_call(
        _kvproj_body,
        grid=(S // tm, MLA_HEADS // hpb),
        in_specs=[pl.BlockSpec((tm, KV_RANK), lambda i, j: (i, COL_CKV // KV_RANK)),
                  pl.BlockSpec((tm, LANES), lambda i, j: (i, COL_KR // LANES)),
                  pl.BlockSpec((1, KV_RANK), lambda i, j: (0, 0)),
                  pl.BlockSpec((KV_RANK, tn), lambda i, j: (0, j)),
                  pl.BlockSpec((tm, LANES), row),
                  pl.BlockSpec((tm, LANES), row),
                  pl.BlockSpec((tm, LANES), row)],
        out_specs=[pl.BlockSpec((hpb, tm, MLA_QK_PAD), lambda i, j: (j, i, 0)),
                   pl.BlockSpec((hpb, tm, MLA_V), lambda i, j: (j, i, 0))],
        out_shape=[jax.ShapeDtypeStruct((MLA_HEADS, S, MLA_QK_PAD), BF16),
                   jax.ShapeDtypeStruct((MLA_HEADS, S, MLA_V), BF16)],
        scratch_shapes=[pltpu.VMEM((tm, KV_RANK), BF16)],
        compiler_params=_cparams(2),
        name="mla_kv_proj",
    )(proj, proj, g.reshape(1, KV_RANK), w, cosf, slo, shi)


ATTN_HEADS_PER_STEP = 2
ATTN_EXP2_SCALE = float(MLA_NOPE + MLA_ROPE) ** -0.5 * 1.4426950408889634
ATTN_STRIP = 64


def _attn_body(q_ref, k_ref, v_ref, z_ref, o_ref, s_scr, p_scr, m_scr, a_scr, acc_scr, *, blk):
    i = pl.program_id(1)
    A, B = 0, 1
    ones_col = (lax.broadcasted_iota(jnp.int32, (blk, MLA_V), 1) == 0).astype(BF16)
    col_chunk = lax.broadcasted_iota(jnp.int32, (ATTN_STRIP, blk), 1) // CHUNK
    row_in_strip = lax.broadcasted_iota(jnp.int32, (ATTN_STRIP, blk), 0)

    def score(h, kb):
        k = k_ref[h, pl.ds(pl.multiple_of(kb * blk, blk), blk), :]
        s_scr[h] = lax.dot_general(q_ref[h], k, (((1,), (1,)), ((), ())), preferred_element_type=F32)

    def softmax(h, diagonal):
        for r in range(blk // ATTN_STRIP):
            rows = slice(r * ATTN_STRIP, (r + 1) * ATTN_STRIP)
            s = s_scr[h, rows, :]
            if diagonal:
                s = jnp.where(col_chunk <= (row_in_strip + r * ATTN_STRIP) // CHUNK, s, NEG)
                m_new = jnp.broadcast_to(jnp.max(s, axis=-1, keepdims=True), (ATTN_STRIP, LANES))
                a_scr[h, rows, :] = jnp.ones((ATTN_STRIP, LANES), F32)
            else:
                m_old = m_scr[h, rows, :]
                m_new = jnp.maximum(m_old, jnp.max(s, axis=-1, keepdims=True))
                a_scr[h, rows, :] = jnp.exp2(m_old - m_new)
            m_scr[h, rows, :] = m_new
            p_scr[h, rows, :] = jnp.exp2(s - jnp.concatenate([m_new] * (blk // LANES), axis=1)).astype(BF16)

    def apply_values(h, kb):
        v = v_ref[h, pl.ds(pl.multiple_of(kb * blk, blk), blk), :]
        pv = jnp.dot(p_scr[h], jnp.concatenate([v, ones_col], axis=1), preferred_element_type=F32)
        a = a_scr[h]
        acc_scr[h] = jnp.concatenate([a, a], axis=1) * acc_scr[h] + pv

    acc_scr[...] = jnp.zeros_like(acc_scr)
    score(A, i)
    softmax(A, True)
    score(B, i)
    softmax(B, True)
    score(A, 0)
    apply_values(A, i)

    def step(j):
        softmax(A, False)
        score(B, j - 1)
        apply_values(B, jnp.where(j == 1, i, j - 2))
        softmax(B, False)
        score(A, j)
        apply_values(A, j - 1)

    def two_steps(t, carry):
        step(2 * t + 1)
        step(2 * t + 2)
        return carry

    lax.fori_loop(0, lax.shift_right_logical(i, 1), two_steps, 0)

    @pl.when(lax.bitwise_and(i, 1) == 1)
    def _():
        step(i)

    apply_values(B, jnp.maximum(i - 1, 0))
    for h in (A, B):
        cols = slice(h * MLA_V, (h + 1) * MLA_V)
        attn = acc_scr[h, :, 0:MLA_V] / acc_scr[h, :, MLA_V:MLA_V + 1]
        o_ref[:, cols] = (attn * _silu(z_ref[:, cols])).astype(o_ref.dtype)


def _attention(q, k, v, proj):
    H, S, _ = q.shape
    blk = min(512, S)
    hp = ATTN_HEADS_PER_STEP
    assert hp == 2
    return pl.pallas_call(
        functools.partial(_attn_body, blk=blk),
        grid=(H // hp, S // blk),
        in_specs=[pl.BlockSpec((hp, blk, MLA_QK_PAD), lambda h, i: (h, i, 0)),
                  pl.BlockSpec((hp, S, MLA_QK_PAD), lambda h, i: (h, 0, 0)),
                  pl.BlockSpec((hp, S, MLA_V), lambda h, i: (h, 0, 0)),
                  pl.BlockSpec((blk, hp * MLA_V), lambda h, i: (i, COL_ZA // (hp * MLA_V) + h))],
        out_specs=pl.BlockSpec((blk, hp * MLA_V), lambda h, i: (i, h)),
        out_shape=jax.ShapeDtypeStruct((S, MLA_WIDTH), BF16),
        scratch_shapes=[pltpu.VMEM((hp, blk, blk), F32), pltpu.VMEM((hp, blk, blk), BF16),
                        pltpu.VMEM((hp, blk, LANES), F32), pltpu.VMEM((hp, blk, LANES), F32),
                        pltpu.VMEM((hp, blk, 2 * MLA_V), F32)],
        compiler_params=_cparams(2),
        name="mla_attention",
    )(q, k, v, proj)


def _sgu_body(u_ref, v_ref, z_ref, lng_ref, lnb_ref, ws_ref, bst_ref, o_ref):
    v = v_ref[...]
    mu = jnp.mean(v, axis=-1, keepdims=True)
    vc = v - mu
    var = jnp.mean(vc * vc, axis=-1, keepdims=True)
    vn = (vc * lax.rsqrt(var + EPS) * lng_ref[...] + lnb_ref[...]).astype(BF16)
    out_chunk = lax.broadcasted_iota(jnp.int32, (SG_BLOCK, SG_BLOCK), 0) // CHUNK
    in_chunk = lax.broadcasted_iota(jnp.int32, (SG_BLOCK, SG_BLOCK), 1) // CHUNK
    causal = in_chunk <= out_chunk
    bst = bst_ref[...]
    for g in range(SG_GROUPS):
        w = jnp.where(causal, ws_ref[g], 0.0).astype(BF16)
        cols = slice(g * SG_GROUP_DIM, (g + 1) * SG_GROUP_DIM)
        bias = bst[:, g:g + 1]
        for b in range(u_ref.shape[0] // SG_BLOCK):
            rows = slice(b * SG_BLOCK, (b + 1) * SG_BLOCK)
            s = jnp.dot(w, vn[rows, cols], preferred_element_type=F32) + bias
            o_ref[rows, cols] = (u_ref[rows, cols] * s * _silu(z_ref[rows, cols])).astype(o_ref.dtype)


def _spatial_gating(proj, ln_g, ln_b, w_s, b_s):
    S = proj.shape[0]
    tm = min(512, S)
    blk = lambda c: pl.BlockSpec((tm, SG_WIDTH), lambda i: (i, c // SG_WIDTH))
    full = lambda shape: pl.BlockSpec(shape, lambda i: (0,) * len(shape))
    return pl.pallas_call(
        _sgu_body,
        grid=(S // tm,),
        in_specs=[blk(COL_SGU), blk(COL_SGV), blk(COL_ZB),
                  full((1, SG_WIDTH)), full((1, SG_WIDTH)),
                  full((SG_GROUPS, SG_BLOCK, SG_BLOCK)), full((SG_BLOCK, SG_GROUPS))],
        out_specs=pl.BlockSpec((tm, SG_WIDTH), lambda i: (i, 0)),
        out_shape=jax.ShapeDtypeStruct((S, SG_WIDTH), BF16),
        compiler_params=_cparams(1),
        name="spatial_gating",
    )(proj, proj, proj, ln_g.reshape(1, SG_WIDTH), ln_b.reshape(1, SG_WIDTH), w_s, jnp.transpose(b_s))


def _mlstm_body(x_ref, v_ref, og_ref, z_ref, if_ref, cw_ref, cb_ref, wq_ref, wk_ref,
                bi_ref, bf_ref, gn_ref, o_ref, xs_scr, ct_scr, n_scr, m_scr, h_scr, *, T, L):
    halo = 8

    @pl.when(pl.program_id(0) == 0)
    def _():
        xs_scr[0:halo, :] = jnp.zeros((halo, ML_WIDTH), F32)
        ct_scr[...] = jnp.zeros_like(ct_scr)
        n_scr[...] = jnp.zeros_like(n_scr)
        m_scr[...] = jnp.zeros_like(m_scr)

    xs_scr[halo:halo + T, :] = x_ref[...]
    cw = cw_ref[...]
    xc = cb_ref[...]
    for kk in range(CONV_K):
        off = halo - (CONV_K - 1) + kk
        xc = xc + cw[kk:kk + 1, :] * xs_scr[off:off + T, :]
    xs_scr[0:halo, :] = xs_scr[T:T + halo, :]
    xcb = _silu(xc).astype(BF16)

    gates = if_ref[...]
    head_lane = lax.broadcasted_iota(jnp.int32, gates.shape, 1) < ML_HEADS
    ig = jnp.where(head_lane, gates + bi_ref[...], 0.0)
    fg = jnp.where(head_lane, pltpu.roll(gates, LANES - ML_HEADS, 1) + bf_ref[...], 0.0)
    lf = jnp.minimum(fg, 0.0) - jnp.log1p(jnp.exp(-jnp.abs(fg)))
    rr = lax.broadcasted_iota(jnp.int32, (L, L), 0)
    cc = lax.broadcasted_iota(jnp.int32, (L, L), 1)
    tril = cc <= rr
    tri_f = tril.astype(F32)
    nc = T // L
    g_l, G_l, a_l, e_l, r_l = [], [], [], [], []
    for c in range(nc):
        rows = slice(c * L, (c + 1) * L)
        g_c = jnp.dot(tri_f, lf[rows], precision=lax.Precision.HIGHEST, preferred_element_type=F32)
        G_c = g_c[L - 1:L, :]
        w_end = G_c - g_c + ig[rows]
        a_c = jnp.max(w_end, axis=0, keepdims=True)
        g_l.append(g_c)
        G_l.append(G_c)
        a_l.append(a_c)
        e_l.append(jnp.exp(w_end - a_c))
        r_l.append(ig[rows] - g_c)
    r_t = jnp.transpose(jnp.concatenate(r_l, axis=0))

    q_l, k_l = [], []
    for h in range(ML_HEADS):
        cols = slice(h * ML_HEAD_DIM, (h + 1) * ML_HEAD_DIM)
        q_l.append(jnp.dot(xcb[:, cols], wq_ref[h], preferred_element_type=F32))
        k_l.append(jnp.dot(xcb[:, cols], wk_ref[h], preferred_element_type=F32) * (ML_HEAD_DIM ** -0.5))

    m_vec = m_scr[0:1, :]
    for c in range(nc):
        rows = slice(c * L, (c + 1) * L)
        g_c, G_c, a_c, e_c = g_l[c], G_l[c], a_l[c], e_l[c]
        m_new = jnp.maximum(G_c + m_vec, a_c)
        sp = jnp.exp(G_c + m_vec - m_new)
        sl = jnp.exp(a_c - m_new)
        for h in range(ML_HEADS):
            cols = slice(h * ML_HEAD_DIM, (h + 1) * ML_HEAD_DIM)
            q_c = q_l[h][rows]
            k_c = k_l[h][rows]
            v_c = v_ref[rows, cols]
            q_cb = q_c.astype(BF16)
            g_col = g_c[:, h:h + 1]
            d = jnp.where(tril, g_col + r_t[h:h + 1, rows], NEG)
            inter_log = g_col + m_vec[:, h:h + 1]
            m_i = jnp.maximum(jnp.max(d, axis=1, keepdims=True), inter_log)
            p = jnp.exp(d - m_i)
            qk = lax.dot_general(q_cb, k_c.astype(BF16), (((1,), (1,)), ((), ())),
                                 preferred_element_type=F32)
            sm = qk * p
            inter_scale = jnp.exp(inter_log - m_i)
            ct = ct_scr[h]
            n_prev = n_scr[h:h + 1, :]
            num = (jnp.dot(sm.astype(BF16), v_c.astype(BF16), preferred_element_type=F32)
                   + inter_scale * jnp.dot(q_cb, ct.astype(BF16), preferred_element_type=F32))
            den = (jnp.sum(sm, axis=1, keepdims=True)
                   + inter_scale * jnp.sum(q_c * n_prev, axis=1, keepdims=True))
            h_scr[rows, cols] = num / jnp.maximum(jnp.abs(den), jnp.exp(-m_i))

            e_col = e_c[:, h:h + 1]
            ev = (e_col * v_c).astype(BF16)
            c_loc_t = jnp.dot(jnp.transpose(k_c).astype(BF16), ev, preferred_element_type=F32)
            sp_h = sp[:, h:h + 1]
            sl_h = sl[:, h:h + 1]
            ct_scr[h] = sp_h * ct + sl_h * c_loc_t
            n_scr[h:h + 1, :] = sp_h * n_prev + sl_h * jnp.sum(e_col * k_c, axis=0, keepdims=True)
        m_vec = m_new
    m_scr[0:1, :] = m_vec

    for h in range(ML_HEADS):
        cols = slice(h * ML_HEAD_DIM, (h + 1) * ML_HEAD_DIM)
        hs = h_scr[:, cols] * _sigmoid(og_ref[:, cols])
        mu = jnp.mean(hs, axis=-1, keepdims=True)
        hc = hs - mu
        var = jnp.mean(hc * hc, axis=-1, keepdims=True)
        y = hc * lax.rsqrt(var + EPS) * gn_ref[:, cols]
        o_ref[:, cols] = (y * _silu(z_ref[:, cols])).astype(o_ref.dtype)


def _mlstm(proj, conv_w, conv_b, w_q, w_k, b_i, b_f, gn_g):
    S = proj.shape[0]
    T = min(512, S)
    L = ML_CHUNK
    wide = lambda c: pl.BlockSpec((T, ML_WIDTH), lambda i: (i, c // ML_WIDTH))
    gate = lambda c: pl.BlockSpec((T, LANES), lambda i: (i, c // LANES))
    full = lambda shape: pl.BlockSpec(shape, lambda i: (0,) * len(shape))
    pad_gate_bias = lambda b: jnp.pad(b, (0, LANES - ML_HEADS)).reshape(1, LANES)
    return pl.pallas_call(
        functools.partial(_mlstm_body, T=T, L=L),
        grid=(S // T,),
        in_specs=[wide(COL_MLX), wide(COL_MLV), wide(COL_MLO), wide(COL_ZC),
                  gate(COL_MLI),
                  full((CONV_K, ML_WIDTH)), full((1, ML_WIDTH)),
                  full((ML_HEADS, ML_HEAD_DIM, ML_HEAD_DIM)), full((ML_HEADS, ML_HEAD_DIM, ML_HEAD_DIM)),
                  full((1, LANES)), full((1, LANES)), full((1, ML_WIDTH))],
        out_specs=pl.BlockSpec((T, ML_WIDTH), lambda i: (i, 0)),
        out_shape=jax.ShapeDtypeStruct((S, ML_WIDTH), BF16),
        scratch_shapes=[pltpu.VMEM((T + 8, ML_WIDTH), F32),
                        pltpu.VMEM((ML_HEADS, ML_HEAD_DIM, ML_HEAD_DIM), F32),
                        pltpu.VMEM((8, ML_HEAD_DIM), F32),
                        pltpu.VMEM((8, LANES), F32),
                        pltpu.VMEM((T, ML_WIDTH), F32)],
        compiler_params=_cparams(1),
        name="mlstm",
    )(proj, proj, proj, proj, proj, conv_w, conv_b.reshape(1, ML_WIDTH), w_q, w_k,
      pad_gate_bias(b_i), pad_gate_bias(b_f), gn_g.reshape(1, ML_WIDTH))


def _merge_body(ya_ref, yb_ref, yc_ref, wa_ref, wb_ref, wc_ref, g0_ref, g1_ref, g2_ref, o_ref, wa_s, wb_s, wc_s):
    @pl.when(pl.program_id(1) == 0)
    def _():
        wa_s[...] = wa_ref[...].astype(BF16)
        wb_s[...] = wb_ref[...].astype(BF16)
        wc_s[...] = wc_ref[...].astype(BF16)

    pa = jnp.dot(ya_ref[...], wa_s[...], preferred_element_type=F32)
    pb = jnp.dot(yb_ref[...], wb_s[...], preferred_element_type=F32)
    pc = jnp.dot(yc_ref[...], wc_s[...], preferred_element_type=F32)
    merged = _sigmoid(g0_ref[...]) * pa + _sigmoid(g1_ref[...]) * pb + _sigmoid(g2_ref[...]) * pc
    o_ref[...] = merged.astype(o_ref.dtype)


def _merge(y_a, y_b, y_c, w_pa, w_pb, w_pc, layer, proj):
    S = y_a.shape[0]
    D = w_pa.shape[-1]
    tm = min(512, S)
    tn = 1024
    act = lambda width: pl.BlockSpec((tm, width), lambda j, i: (i, 0))
    wgt = lambda width: pl.BlockSpec((None, width, tn), lambda j, i: (layer, 0, j), pipeline_mode=pl.Buffered(1))
    gate = lambda b: pl.BlockSpec((tm, tn), lambda j, i: (i, (COL_GATE + b * D) // tn + j))
    return pl.pallas_call(
        _merge_body,
        grid=(D // tn, S // tm),
        in_specs=[act(MLA_WIDTH), act(SG_WIDTH), act(ML_WIDTH),
                  wgt(MLA_WIDTH), wgt(SG_WIDTH), wgt(ML_WIDTH),
                  gate(0), gate(1), gate(2)],
        out_specs=pl.BlockSpec((tm, tn), lambda j, i: (i, j)),
        out_shape=jax.ShapeDtypeStruct((S, D), BF16),
        scratch_shapes=[pltpu.VMEM((MLA_WIDTH, tn), BF16), pltpu.VMEM((SG_WIDTH, tn), BF16),
                        pltpu.VMEM((ML_WIDTH, tn), BF16)],
        compiler_params=_cparams(2),
        name="gated_merge",
    )(y_a, y_b, y_c, w_pa, w_pb, w_pc, proj, proj, proj)


def _outproj_body(m_ref, w_ref, x_ref, o_ref, w_s):
    @pl.when(pl.program_id(1) == 0)
    def _():
        w_s[...] = w_ref[...].astype(BF16)

    o_ref[...] = x_ref[...] + jnp.dot(m_ref[...], w_s[...], preferred_element_type=F32)


def _out_proj(merged, w_out, layer, x):
    S, D = x.shape
    tm = min(512, S)
    tn = 1024
    return pl.pallas_call(
        _outproj_body,
        grid=(D // tn, S // tm),
        in_specs=[pl.BlockSpec((tm, D), lambda j, i: (i, 0)),
                  pl.BlockSpec((None, D, tn), lambda j, i: (layer, 0, j), pipeline_mode=pl.Buffered(1)),
                  pl.BlockSpec((tm, tn), lambda j, i: (i, j))],
        out_specs=pl.BlockSpec((tm, tn), lambda j, i: (i, j)),
        out_shape=jax.ShapeDtypeStruct((S, D), F32),
        scratch_shapes=[pltpu.VMEM((D, tn), BF16)],
        compiler_params=_cparams(2),
        name="out_proj",
    )(merged, w_out, x)


def _pack_w_uq(w_uq):
    w = w_uq.reshape(Q_RANK, MLA_HEADS, MLA_NOPE + MLA_ROPE).astype(BF16)
    nope = w[:, :, :MLA_NOPE].reshape(Q_RANK, MLA_HEADS * MLA_NOPE)
    rope = w[:, :, MLA_NOPE:].reshape(Q_RANK, MLA_HEADS * MLA_ROPE)
    return jnp.concatenate([nope, rope], axis=1)


def _rope_tables(S):
    half = MLA_ROPE // 2
    inv_freq = ROPE_THETA ** (-jnp.arange(0, MLA_ROPE, 2, dtype=F32) / MLA_ROPE)
    ang = jnp.arange(S, dtype=F32)[:, None] * inv_freq[None, :]
    cos, sin = jnp.cos(ang), jnp.sin(ang)
    z = lambda n: jnp.zeros((S, n), F32)
    cosf = jnp.concatenate([cos, cos, z(LANES - MLA_ROPE)], axis=1)
    sin_lo = jnp.concatenate([-sin, z(LANES - half)], axis=1)
    sin_hi = jnp.concatenate([z(half), sin, z(LANES - MLA_ROPE)], axis=1)
    return cosf, sin_lo, sin_hi


def _layer(x, tables, layer, norm_g, w_in_t, mla_gq, mla_gkv, w_uq_p, w_ukv_b, sg_ln_g, sg_ln_b, sg_ws, sg_bs,
           ml_conv_w, ml_conv_b, ml_wq_b, ml_wk_b, ml_bi, ml_bf, ml_gn_g, w_pa, w_pb, w_pc, w_out):
    h = _rmsnorm(x, norm_g, BF16)
    proj = _in_proj(h, w_in_t, layer)
    q = _q_proj(proj, mla_gq, w_uq_p, *tables)
    k, v = _kv_proj(proj, mla_gkv, w_ukv_b, *tables)
    y_a = _attention(q, k, v, proj)
    y_b = _spatial_gating(proj, sg_ln_g, sg_ln_b, sg_ws, sg_bs)
    y_c = _mlstm(proj, ml_conv_w, ml_conv_b, ml_wq_b, ml_wk_b, ml_bi, ml_bf, ml_gn_g)
    merged = _merge(y_a, y_b, y_c, w_pa, w_pb, w_pc, layer, proj)
    return _out_proj(merged, w_out, layer, x)


def kernel(x, norm_g, w_in, mla_gq, mla_gkv, mla_wuq, mla_wukv, sg_ln_g, sg_ln_b, sg_ws, sg_bs, ml_conv_w, ml_conv_b, ml_wq, ml_wk, ml_bi, ml_bf, ml_gn_g, w_pa, w_pb, w_pc, w_out, final_g):
    B, S, D = x.shape
    depth = w_in.shape[0]
    tables = _rope_tables(S)
    cast = lambda w: w.astype(BF16)
    w_in_t = jnp.swapaxes(w_in, 1, 2)
    outs = []
    for b in range(B):
        xb = x[b]
        for l in range(depth):
            xb = _layer(xb, tables, l, norm_g[l], w_in_t, mla_gq[l], mla_gkv[l],
                        _pack_w_uq(mla_wuq[l]), cast(mla_wukv[l]),
                        sg_ln_g[l], sg_ln_b[l], sg_ws[l], sg_bs[l], ml_conv_w[l], ml_conv_b[l],
                        cast(ml_wq[l]), cast(ml_wk[l]), ml_bi[l], ml_bf[l], ml_gn_g[l],
                        w_pa, w_pb, w_pc, w_out)
        outs.append(_rmsnorm(xb, final_g, x.dtype))
    return outs[0][None] if B == 1 else jnp.stack(outs, axis=0)
```

```python
import functools

import jax
import jax.numpy as jnp
from jax import lax
from jax.experimental import pallas as pl
from jax.experimental.pallas import tpu as pltpu

F32 = jnp.float32
BF16 = jnp.bfloat16

D_MODEL = 4096
CHUNK = 64
EPS = 1e-6

MLA_HEADS = 16
MLA_NOPE = 128
MLA_ROPE = 64
MLA_V = 128
Q_RANK = 1024
KV_RANK = 512
ROPE_THETA = 10000.0
MLA_WIDTH = MLA_HEADS * MLA_V
MLA_QK_PAD = 256

SG_BLOCK = 128
SG_GROUPS = 4
SG_WIDTH = 1024
SG_GROUP_DIM = SG_WIDTH // SG_GROUPS

ML_HEADS = 4
ML_WIDTH = 1024
ML_HEAD_DIM = 256
CONV_K = 4
ML_CHUNK = 128

LANES = 128
NEG = -1e30

COL_CQ = 0
COL_ZA = 1024
COL_SGU = 3072
COL_SGV = 4096
COL_ZB = 5120
COL_MLX = 6144
COL_MLV = 7168
COL_MLO = 8192
COL_ZC = 9216
COL_GATE = 10240
COL_CKV = 22528
COL_KR = 23040
COL_MLI = 23168
IN_PAD = 23552

VMEM_LIMIT = 56 * 1024 * 1024


def _cparams(n_axes, vmem=VMEM_LIMIT):
    return pltpu.CompilerParams(dimension_semantics=("arbitrary",) * n_axes,
                                vmem_limit_bytes=vmem)


def _sigmoid(x):
    return 1.0 / (1.0 + jnp.exp(-x))


def _silu(x):
    return x * _sigmoid(x)


def _rmsnorm_body(x_ref, g_ref, o_ref):
    x = x_ref[...]
    y = x * lax.rsqrt(jnp.mean(x * x, axis=-1, keepdims=True) + EPS)
    o_ref[...] = (y * g_ref[...]).astype(o_ref.dtype)


def _rmsnorm(x, g, out_dtype):
    S, D = x.shape
    tm = min(512, S)
    return pl.pallas_call(
        _rmsnorm_body,
        grid=(S // tm,),
        in_specs=[pl.BlockSpec((tm, D), lambda i: (i, 0)),
                  pl.BlockSpec((1, D), lambda i: (0, 0))],
        out_specs=pl.BlockSpec((tm, D), lambda i: (i, 0)),
        out_shape=jax.ShapeDtypeStruct((S, D), out_dtype),
        compiler_params=_cparams(1),
        name="rmsnorm",
    )(x, g.reshape(1, D))


IN_TN = 1024
IN_CHUNK = 128
IN_CHUNKS_PER_TILE = IN_TN // IN_CHUNK

_IN_TILE_SRC = (0, 1600, 2624, 3648, 4672, 5696, 6720, 7744, 8768, 9792) + tuple(
    10824 + IN_TN * t for t in range(3 * D_MODEL // IN_TN))
_IN_CHUNK_SRC = tuple(src + IN_CHUNK * c for src in _IN_TILE_SRC for c in range(IN_CHUNKS_PER_TILE)) + (
    1024, 1152, 1280, 1408, 1536, 10816, 0, 0)
assert len(_IN_CHUNK_SRC) * IN_CHUNK == IN_PAD


def _in_proj_body(src_ref, h_ref, wt_hbm, o_ref, stage, wb, sem, *, layer, chunks_per_step):
    j = pl.program_id(0)
    i = pl.program_id(1)
    slot = lax.rem(j, 2)

    def chunk_copy(tile, c, buf):
        row = pl.multiple_of(src_ref[tile * IN_CHUNKS_PER_TILE + c], 8)
        return pltpu.make_async_copy(wt_hbm.at[layer, pl.ds(row, IN_CHUNK), :], stage.at[buf], sem.at[buf])

    def start_fetch(tile, chunks):
        chunk_copy(tile, chunks[0], 0).start()

    def finish_fetch(tile, dst_slot, chunks):
        for n, c in enumerate(chunks):
            if n + 1 < len(chunks):
                chunk_copy(tile, chunks[n + 1], (n + 1) % 2).start()
            chunk_copy(tile, c, n % 2).wait()
            start = c * IN_CHUNK if isinstance(c, int) else pl.multiple_of(c * IN_CHUNK, IN_CHUNK)
            wb[dst_slot, pl.ds(start, IN_CHUNK), :] = stage[n % 2].astype(BF16)

    @pl.when(jnp.logical_and(j == 0, i == 0))
    def _():
        first_tile = list(range(IN_CHUNKS_PER_TILE))
        start_fetch(0, first_tile)
        finish_fetch(0, 0, first_tile)

    prefetch = jnp.logical_and(j + 1 < pl.num_programs(0), i * chunks_per_step < IN_CHUNKS_PER_TILE)
    share = [i * chunks_per_step + n for n in range(chunks_per_step)]

    @pl.when(prefetch)
    def _():
        start_fetch(j + 1, share)

    o_ref[...] = lax.dot_general(h_ref[...], wb[slot], (((1,), (1,)), ((), ())), preferred_element_type=F32)

    @pl.when(prefetch)
    def _():
        finish_fetch(j + 1, 1 - slot, share)


def _in_proj(h, w_in_t, layer):
    S, K = h.shape
    tm = min(1024, S)
    n_row_tiles = S // tm
    chunks_per_step = -(-IN_CHUNKS_PER_TILE // n_row_tiles)
    assert IN_CHUNKS_PER_TILE % chunks_per_step == 0
    grid_spec = pltpu.PrefetchScalarGridSpec(
        num_scalar_prefetch=1,
        grid=(IN_PAD // IN_TN, n_row_tiles),
        in_specs=[pl.BlockSpec((tm, K), lambda j, i, src: (i, 0)),
                  pl.BlockSpec(memory_space=pl.ANY)],
        out_specs=pl.BlockSpec((tm, IN_TN), lambda j, i, src: (i, j)),
        scratch_shapes=[pltpu.VMEM((2, IN_CHUNK, K), F32), pltpu.VMEM((2, IN_TN, K), BF16),
                        pltpu.SemaphoreType.DMA((2,))])
    return pl.pallas_call(
        functools.partial(_in_proj_body, layer=layer, chunks_per_step=chunks_per_step),
        grid_spec=grid_spec,
        out_shape=jax.ShapeDtypeStruct((S, IN_PAD), F32),
        compiler_params=_cparams(2),
        name="in_proj",
    )(jnp.asarray(_IN_CHUNK_SRC, jnp.int32), h, w_in_t)


PROJ_ROW_PARTS = 2


def _rope(seg, cosf, sin_lo, sin_hi):
    return (seg * cosf
            + pltpu.roll(seg, LANES - MLA_ROPE // 2, 1) * sin_lo
            + pltpu.roll(seg, MLA_ROPE // 2, 1) * sin_hi)


def _qproj_body(cq_ref, g_ref, wn_ref, wr_ref, cos_ref, slo_ref, shi_ref, o_ref, a_scr):
    @pl.when(pl.program_id(1) == 0)
    def _():
        c = cq_ref[...]
        y = c * lax.rsqrt(jnp.mean(c * c, axis=-1, keepdims=True) + EPS)
        a_scr[...] = (y * g_ref[...]).astype(BF16)

    part = a_scr.shape[0] // PROJ_ROW_PARTS
    for r in range(PROJ_ROW_PARTS):
        rows = slice(r * part, (r + 1) * part)
        nope = jnp.dot(a_scr[rows, :], wn_ref[...], preferred_element_type=F32) * ATTN_EXP2_SCALE
        rope = jnp.dot(a_scr[rows, :], wr_ref[...], preferred_element_type=F32) * ATTN_EXP2_SCALE
        cosf, slo, shi = cos_ref[rows, :], slo_ref[rows, :], shi_ref[rows, :]
        for hh in range(o_ref.shape[0]):
            o_ref[hh, rows, 0:MLA_NOPE] = nope[:, hh * MLA_NOPE:(hh + 1) * MLA_NOPE].astype(BF16)
            pair = rope[:, (hh // 2) * LANES:(hh // 2 + 1) * LANES]
            seg = pair if hh % 2 == 0 else pltpu.roll(pair, MLA_ROPE, 1)
            o_ref[hh, rows, MLA_NOPE:MLA_QK_PAD] = _rope(seg, cosf, slo, shi).astype(BF16)


def _q_proj(proj, g, w, cosf, slo, shi):
    S = proj.shape[0]
    tm = min(1024, S)
    hpb = 8
    rope0 = MLA_HEADS * MLA_NOPE // (hpb * MLA_ROPE)
    row = lambda i, j: (i, 0)
    return pl.pallas_call(
        _qproj_body,
        grid=(S // tm, MLA_HEADS // hpb),
        in_specs=[pl.BlockSpec((tm, Q_RANK), lambda i, j: (i, COL_CQ // Q_RANK)),
                  pl.BlockSpec((1, Q_RANK), lambda i, j: (0, 0)),
                  pl.BlockSpec((Q_RANK, hpb * MLA_NOPE), lambda i, j: (0, j)),
                  pl.BlockSpec((Q_RANK, hpb * MLA_ROPE), lambda i, j: (0, rope0 + j)),
                  pl.BlockSpec((tm, LANES), row),
                  pl.BlockSpec((tm, LANES), row),
                  pl.BlockSpec((tm, LANES), row)],
        out_specs=pl.BlockSpec((hpb, tm, MLA_QK_PAD), lambda i, j: (j, i, 0)),
        out_shape=jax.ShapeDtypeStruct((MLA_HEADS, S, MLA_QK_PAD), BF16),
        scratch_shapes=[pltpu.VMEM((tm, Q_RANK), BF16)],
        compiler_params=_cparams(2),
        name="mla_q_proj",
    )(proj, g.reshape(1, Q_RANK), w, w, cosf, slo, shi)


def _kvproj_body(ckv_ref, kr_ref, g_ref, w_ref, cos_ref, slo_ref, shi_ref, k_ref, v_ref, a_scr):
    @pl.when(pl.program_id(1) == 0)
    def _():
        c = ckv_ref[...]
        y = c * lax.rsqrt(jnp.mean(c * c, axis=-1, keepdims=True) + EPS)
        a_scr[...] = (y * g_ref[...]).astype(BF16)

    width = MLA_NOPE + MLA_V
    part = a_scr.shape[0] // PROJ_ROW_PARTS
    for r in range(PROJ_ROW_PARTS):
        rows = slice(r * part, (r + 1) * part)
        res = jnp.dot(a_scr[rows, :], w_ref[...], preferred_element_type=F32)
        k_rot = _rope(kr_ref[rows, :], cos_ref[rows, :], slo_ref[rows, :], shi_ref[rows, :]).astype(BF16)
        for hh in range(k_ref.shape[0]):
            base = hh * width
            k_ref[hh, rows, 0:MLA_NOPE] = res[:, base:base + MLA_NOPE].astype(BF16)
            k_ref[hh, rows, MLA_NOPE:MLA_QK_PAD] = k_rot
            v_ref[hh, rows, :] = res[:, base + MLA_NOPE:base + width].astype(BF16)


def _kv_proj(proj, g, w, cosf, slo, shi):
    S = proj.shape[0]
    tm = min(1024, S)
    hpb = 8
    tn = hpb * (MLA_NOPE + MLA_V)
    row = lambda i, j: (i, 0)
    return pl.pallas_call(
        _kvproj_body,
        grid=(S // tm, MLA_HEADS // hpb),
        in_specs=[pl.BlockSpec((tm, KV_RANK), lambda i, j: (i, COL_CKV // KV_RANK)),
                  pl.BlockSpec((tm, LANES), lambda i, j: (i, COL_KR // LANES)),
                  pl.BlockSpec((1, KV_RANK), lambda i, j: (0, 0)),
                  pl.BlockSpec((KV_RANK, tn), lambda i, j: (0, j)),
                  pl.BlockSpec((tm, LANES), row),
                  pl.BlockSpec((tm, LANES), row),
                  pl.BlockSpec((tm, LANES), row)],
        out_specs=[pl.BlockSpec((hpb, tm, MLA_QK_PAD), lambda i, j: (j, i, 0)),
                   pl.BlockSpec((hpb, tm, MLA_V), lambda i, j: (j, i, 0))],
        out_shape=[jax.ShapeDtypeStruct((MLA_HEADS, S, MLA_QK_PAD), BF16),
                   jax.ShapeDtypeStruct((MLA_HEADS, S, MLA_V), BF16)],
        scratch_shapes=[pltpu.VMEM((tm, KV_RANK), BF16)],
        compiler_params=_cparams(2),
        name="mla_kv_proj",
    )(proj, proj, g.reshape(1, KV_RANK), w, cosf, slo, shi)


ATTN_HEADS_PER_STEP = 2
ATTN_EXP2_SCALE = float(MLA_NOPE + MLA_ROPE) ** -0.5 * 1.4426950408889634
ATTN_STRIP = 64
ATTN_LOG2_STEPS_PER_TRIP = 2


def _attn_body(q_ref, k_ref, v_ref, z_ref, o_ref, s_scr, p_scr, m_scr, a_scr, acc_scr, *, blk):
    i = pl.program_id(1)
    A, B = 0, 1
    ones_col = (lax.broadcasted_iota(jnp.int32, (blk, MLA_V), 1) == 0).astype(BF16)
    col_chunk = lax.broadcasted_iota(jnp.int32, (ATTN_STRIP, blk), 1) // CHUNK
    row_in_strip = lax.broadcasted_iota(jnp.int32, (ATTN_STRIP, blk), 0)

    def score(h, kb):
        k = k_ref[h, pl.ds(pl.multiple_of(kb * blk, blk), blk), :]
        s_scr[h] = lax.dot_general(q_ref[h], k, (((1,), (1,)), ((), ())), preferred_element_type=F32)

    def softmax(h, diagonal):
        for r in range(blk // ATTN_STRIP):
            rows = slice(r * ATTN_STRIP, (r + 1) * ATTN_STRIP)
            s = s_scr[h, rows, :]
            if diagonal:
                s = jnp.where(col_chunk <= (row_in_strip + r * ATTN_STRIP) // CHUNK, s, NEG)
                m_new = jnp.broadcast_to(jnp.max(s, axis=-1, keepdims=True), (ATTN_STRIP, LANES))
                a_scr[h, rows, :] = jnp.ones((ATTN_STRIP, LANES), F32)
            else:
                m_old = m_scr[h, rows, :]
                m_new = jnp.maximum(m_old, jnp.max(s, axis=-1, keepdims=True))
                a_scr[h, rows, :] = jnp.exp2(m_old - m_new)
            m_scr[h, rows, :] = m_new
            p_scr[h, rows, :] = jnp.exp2(s - jnp.concatenate([m_new] * (blk // LANES), axis=1)).astype(BF16)

    def apply_values(h, kb):
        v = v_ref[h, pl.ds(pl.multiple_of(kb * blk, blk), blk), :]
        pv = jnp.dot(p_scr[h], jnp.concatenate([v, ones_col], axis=1), preferred_element_type=F32)
        a = a_scr[h]
        acc_scr[h] = jnp.concatenate([a, a], axis=1) * acc_scr[h] + pv

    acc_scr[...] = jnp.zeros_like(acc_scr)
    score(A, i)
    softmax(A, True)
    score(B, i)
    softmax(B, True)
    score(A, 0)
    apply_values(A, i)

    def step(j):
        softmax(A, False)
        score(B, j - 1)
        apply_values(B, jnp.where(j == 1, i, j - 2))
        softmax(B, False)
        score(A, j)
        apply_values(A, j - 1)

    full_trips = lax.shift_right_logical(i, ATTN_LOG2_STEPS_PER_TRIP)
    per_trip = 1 << ATTN_LOG2_STEPS_PER_TRIP

    def trip(t, carry):
        for n in range(per_trip):
            step(per_trip * t + 1 + n)
        return carry

    lax.fori_loop(0, full_trips, trip, 0)

    done = per_trip * full_trips
    for bit in reversed(range(ATTN_LOG2_STEPS_PER_TRIP)):
        group = 1 << bit
        has_group = lax.bitwise_and(i, group) != 0

        @pl.when(has_group)
        def _(done=done, group=group):
            for n in range(group):
                step(done + 1 + n)

        done = done + jnp.where(has_group, group, 0)

    apply_values(B, jnp.maximum(i - 1, 0))
    for h in (A, B):
        cols = slice(h * MLA_V, (h + 1) * MLA_V)
        attn = acc_scr[h, :, 0:MLA_V] / acc_scr[h, :, MLA_V:MLA_V + 1]
        o_ref[:, cols] = (attn * _silu(z_ref[:, cols])).astype(o_ref.dtype)


def _attention(q, k, v, proj):
    H, S, _ = q.shape
    blk = min(512, S)
    hp = ATTN_HEADS_PER_STEP
    assert hp == 2
    return pl.pallas_call(
        functools.partial(_attn_body, blk=blk),
        grid=(H // hp, S // blk),
        in_specs=[pl.BlockSpec((hp, blk, MLA_QK_PAD), lambda h, i: (h, i, 0)),
                  pl.BlockSpec((hp, S, MLA_QK_PAD), lambda h, i: (h, 0, 0)),
                  pl.BlockSpec((hp, S, MLA_V), lambda h, i: (h, 0, 0)),
                  pl.BlockSpec((blk, hp * MLA_V), lambda h, i: (i, COL_ZA // (hp * MLA_V) + h))],
        out_specs=pl.BlockSpec((blk, hp * MLA_V), lambda h, i: (i, h)),
        out_shape=jax.ShapeDtypeStruct((S, MLA_WIDTH), BF16),
        scratch_shapes=[pltpu.VMEM((hp, blk, blk), F32), pltpu.VMEM((hp, blk, blk), BF16),
                        pltpu.VMEM((hp, blk, LANES), F32), pltpu.VMEM((hp, blk, LANES), F32),
                        pltpu.VMEM((hp, blk, 2 * MLA_V), F32)],
        compiler_params=_cparams(2),
        name="mla_attention",
    )(q, k, v, proj)


def _sgu_body(u_ref, v_ref, z_ref, lng_ref, lnb_ref, ws_ref, bst_ref, o_ref):
    v = v_ref[...]
    mu = jnp.mean(v, axis=-1, keepdims=True)
    vc = v - mu
    var = jnp.mean(vc * vc, axis=-1, keepdims=True)
    vn = (vc * lax.rsqrt(var + EPS) * lng_ref[...] + lnb_ref[...]).astype(BF16)
    out_chunk = lax.broadcasted_iota(jnp.int32, (SG_BLOCK, SG_BLOCK), 0) // CHUNK
    in_chunk = lax.broadcasted_iota(jnp.int32, (SG_BLOCK, SG_BLOCK), 1) // CHUNK
    causal = in_chunk <= out_chunk
    bst = bst_ref[...]
    for g in range(SG_GROUPS):
        w = jnp.where(causal, ws_ref[g], 0.0).astype(BF16)
        cols = slice(g * SG_GROUP_DIM, (g + 1) * SG_GROUP_DIM)
        bias = bst[:, g:g + 1]
        for b in range(u_ref.shape[0] // SG_BLOCK):
            rows = slice(b * SG_BLOCK, (b + 1) * SG_BLOCK)
            s = jnp.dot(w, vn[rows, cols], preferred_element_type=F32) + bias
            o_ref[rows, cols] = (u_ref[rows, cols] * s * _silu(z_ref[rows, cols])).astype(o_ref.dtype)


def _spatial_gating(proj, ln_g, ln_b, w_s, b_s):
    S = proj.shape[0]
    tm = min(512, S)
    blk = lambda c: pl.BlockSpec((tm, SG_WIDTH), lambda i: (i, c // SG_WIDTH))
    full = lambda shape: pl.BlockSpec(shape, lambda i: (0,) * len(shape))
    return pl.pallas_call(
        _sgu_body,
        grid=(S // tm,),
        in_specs=[blk(COL_SGU), blk(COL_SGV), blk(COL_ZB),
                  full((1, SG_WIDTH)), full((1, SG_WIDTH)),
                  full((SG_GROUPS, SG_BLOCK, SG_BLOCK)), full((SG_BLOCK, SG_GROUPS))],
        out_specs=pl.BlockSpec((tm, SG_WIDTH), lambda i: (i, 0)),
        out_shape=jax.ShapeDtypeStruct((S, SG_WIDTH), BF16),
        compiler_params=_cparams(1),
        name="spatial_gating",
    )(proj, proj, proj, ln_g.reshape(1, SG_WIDTH), ln_b.reshape(1, SG_WIDTH), w_s, jnp.transpose(b_s))


def _mlstm_body(x_ref, v_ref, og_ref, z_ref, if_ref, cw_ref, cb_ref, wq_ref, wk_ref,
                bi_ref, bf_ref, gn_ref, o_ref, xs_scr, ct_scr, n_scr, m_scr, h_scr, *, T, L):
    halo = 8

    @pl.when(pl.program_id(0) == 0)
    def _():
        xs_scr[0:halo, :] = jnp.zeros((halo, ML_WIDTH), F32)
        ct_scr[...] = jnp.zeros_like(ct_scr)
        n_scr[...] = jnp.zeros_like(n_scr)
        m_scr[...] = jnp.zeros_like(m_scr)

    xs_scr[halo:halo + T, :] = x_ref[...]
    cw = cw_ref[...]
    xc = cb_ref[...]
    for kk in range(CONV_K):
        off = halo - (CONV_K - 1) + kk
        xc = xc + cw[kk:kk + 1, :] * xs_scr[off:off + T, :]
    xs_scr[0:halo, :] = xs_scr[T:T + halo, :]
    xcb = _silu(xc).astype(BF16)

    gates = if_ref[...]
    head_lane = lax.broadcasted_iota(jnp.int32, gates.shape, 1) < ML_HEADS
    ig = jnp.where(head_lane, gates + bi_ref[...], 0.0)
    fg = jnp.where(head_lane, pltpu.roll(gates, LANES - ML_HEADS, 1) + bf_ref[...], 0.0)
    lf = jnp.minimum(fg, 0.0) - jnp.log1p(jnp.exp(-jnp.abs(fg)))
    rr = lax.broadcasted_iota(jnp.int32, (L, L), 0)
    cc = lax.broadcasted_iota(jnp.int32, (L, L), 1)
    tril = cc <= rr
    tri_f = tril.astype(F32)
    nc = T // L
    g_l, G_l, a_l, e_l, r_l = [], [], [], [], []
    for c in range(nc):
        rows = slice(c * L, (c + 1) * L)
        g_c = jnp.dot(tri_f, lf[rows], precision=lax.Precision.HIGHEST, preferred_element_type=F32)
        G_c = g_c[L - 1:L, :]
        w_end = G_c - g_c + ig[rows]
        a_c = jnp.max(w_end, axis=0, keepdims=True)
        g_l.append(g_c)
        G_l.append(G_c)
        a_l.append(a_c)
        e_l.append(jnp.exp(w_end - a_c))
        r_l.append(ig[rows] - g_c)
    r_t = jnp.transpose(jnp.concatenate(r_l, axis=0))

    q_l, k_l = [], []
    for h in range(ML_HEADS):
        cols = slice(h * ML_HEAD_DIM, (h + 1) * ML_HEAD_DIM)
        q_l.append(jnp.dot(xcb[:, cols], wq_ref[h], preferred_element_type=F32))
        k_l.append(jnp.dot(xcb[:, cols], wk_ref[h], preferred_element_type=F32) * (ML_HEAD_DIM ** -0.5))

    m_vec = m_scr[0:1, :]
    for c in range(nc):
        rows = slice(c * L, (c + 1) * L)
        g_c, G_c, a_c, e_c = g_l[c], G_l[c], a_l[c], e_l[c]
        m_new = jnp.maximum(G_c + m_vec, a_c)
        sp = jnp.exp(G_c + m_vec - m_new)
        sl = jnp.exp(a_c - m_new)
        for h in range(ML_HEADS):
            cols = slice(h * ML_HEAD_DIM, (h + 1) * ML_HEAD_DIM)
            q_c = q_l[h][rows]
            k_c = k_l[h][rows]
            v_c = v_ref[rows, cols]
            q_cb = q_c.astype(BF16)
            g_col = g_c[:, h:h + 1]
            d = jnp.where(tril, g_col + r_t[h:h + 1, rows], NEG)
            inter_log = g_col + m_vec[:, h:h + 1]
            m_i = jnp.maximum(jnp.max(d, axis=1, keepdims=True), inter_log)
            p = jnp.exp(d - m_i)
            qk = lax.dot_general(q_cb, k_c.astype(BF16), (((1,), (1,)), ((), ())),
                                 preferred_element_type=F32)
            sm = qk * p
            inter_scale = jnp.exp(inter_log - m_i)
            ct = ct_scr[h]
            n_prev = n_scr[h:h + 1, :]
            num = (jnp.dot(sm.astype(BF16), v_c.astype(BF16), preferred_element_type=F32)
                   + inter_scale * jnp.dot(q_cb, ct.astype(BF16), preferred_element_type=F32))
            den = (jnp.sum(sm, axis=1, keepdims=True)
                   + inter_scale * jnp.sum(q_c * n_prev, axis=1, keepdims=True))
            h_scr[rows, cols] = num / jnp.maximum(jnp.abs(den), jnp.exp(-m_i))

            e_col = e_c[:, h:h + 1]
            ev = (e_col * v_c).astype(BF16)
            c_loc_t = jnp.dot(jnp.transpose(k_c).astype(BF16), ev, preferred_element_type=F32)
            sp_h = sp[:, h:h + 1]
            sl_h = sl[:, h:h + 1]
            ct_scr[h] = sp_h * ct + sl_h * c_loc_t
            n_scr[h:h + 1, :] = sp_h * n_prev + sl_h * jnp.sum(e_col * k_c, axis=0, keepdims=True)
        m_vec = m_new
    m_scr[0:1, :] = m_vec

    for h in range(ML_HEADS):
        cols = slice(h * ML_HEAD_DIM, (h + 1) * ML_HEAD_DIM)
        hs = h_scr[:, cols] * _sigmoid(og_ref[:, cols])
        mu = jnp.mean(hs, axis=-1, keepdims=True)
        hc = hs - mu
        var = jnp.mean(hc * hc, axis=-1, keepdims=True)
        y = hc * lax.rsqrt(var + EPS) * gn_ref[:, cols]
        o_ref[:, cols] = (y * _silu(z_ref[:, cols])).astype(o_ref.dtype)


def _mlstm(proj, conv_w, conv_b, w_q, w_k, b_i, b_f, gn_g):
    S = proj.shape[0]
    T = min(512, S)
    L = ML_CHUNK
    wide = lambda c: pl.BlockSpec((T, ML_WIDTH), lambda i: (i, c // ML_WIDTH))
    gate = lambda c: pl.BlockSpec((T, LANES), lambda i: (i, c // LANES))
    full = lambda shape: pl.BlockSpec(shape, lambda i: (0,) * len(shape))
    pad_gate_bias = lambda b: jnp.pad(b, (0, LANES - ML_HEADS)).reshape(1, LANES)
    return pl.pallas_call(
        functools.partial(_mlstm_body, T=T, L=L),
        grid=(S // T,),
        in_specs=[wide(COL_MLX), wide(COL_MLV), wide(COL_MLO), wide(COL_ZC),
                  gate(COL_MLI),
                  full((CONV_K, ML_WIDTH)), full((1, ML_WIDTH)),
                  full((ML_HEADS, ML_HEAD_DIM, ML_HEAD_DIM)), full((ML_HEADS, ML_HEAD_DIM, ML_HEAD_DIM)),
                  full((1, LANES)), full((1, LANES)), full((1, ML_WIDTH))],
        out_specs=pl.BlockSpec((T, ML_WIDTH), lambda i: (i, 0)),
        out_shape=jax.ShapeDtypeStruct((S, ML_WIDTH), BF16),
        scratch_shapes=[pltpu.VMEM((T + 8, ML_WIDTH), F32),
                        pltpu.VMEM((ML_HEADS, ML_HEAD_DIM, ML_HEAD_DIM), F32),
                        pltpu.VMEM((8, ML_HEAD_DIM), F32),
                        pltpu.VMEM((8, LANES), F32),
                        pltpu.VMEM((T, ML_WIDTH), F32)],
        compiler_params=_cparams(1),
        name="mlstm",
    )(proj, proj, proj, proj, proj, conv_w, conv_b.reshape(1, ML_WIDTH), w_q, w_k,
      pad_gate_bias(b_i), pad_gate_bias(b_f), gn_g.reshape(1, ML_WIDTH))


def _merge_body(ya_ref, yb_ref, yc_ref, wa_ref, wb_ref, wc_ref, g0_ref, g1_ref, g2_ref, o_ref, wa_s, wb_s, wc_s):
    @pl.when(pl.program_id(1) == 0)
    def _():
        wa_s[...] = wa_ref[...].astype(BF16)
        wb_s[...] = wb_ref[...].astype(BF16)
        wc_s[...] = wc_ref[...].astype(BF16)

    pa = jnp.dot(ya_ref[...], wa_s[...], preferred_element_type=F32)
    pb = jnp.dot(yb_ref[...], wb_s[...], preferred_element_type=F32)
    pc = jnp.dot(yc_ref[...], wc_s[...], preferred_element_type=F32)
    merged = _sigmoid(g0_ref[...]) * pa + _sigmoid(g1_ref[...]) * pb + _sigmoid(g2_ref[...]) * pc
    o_ref[...] = merged.astype(o_ref.dtype)


def _merge(y_a, y_b, y_c, w_pa, w_pb, w_pc, layer, proj):
    S = y_a.shape[0]
    D = w_pa.shape[-1]
    tm = min(512, S)
    tn = 1024
    act = lambda width: pl.BlockSpec((tm, width), lambda j, i: (i, 0))
    wgt = lambda width: pl.BlockSpec((None, width, tn), lambda j, i: (layer, 0, j), pipeline_mode=pl.Buffered(1))
    gate = lambda b: pl.BlockSpec((tm, tn), lambda j, i: (i, (COL_GATE + b * D) // tn + j))
    return pl.pallas_call(
        _merge_body,
        grid=(D // tn, S // tm),
        in_specs=[act(MLA_WIDTH), act(SG_WIDTH), act(ML_WIDTH),
                  wgt(MLA_WIDTH), wgt(SG_WIDTH), wgt(ML_WIDTH),
                  gate(0), gate(1), gate(2)],
        out_specs=pl.BlockSpec((tm, tn), lambda j, i: (i, j)),
        out_shape=jax.ShapeDtypeStruct((S, D), BF16),
        scratch_shapes=[pltpu.VMEM((MLA_WIDTH, tn), BF16), pltpu.VMEM((SG_WIDTH, tn), BF16),
                        pltpu.VMEM((ML_WIDTH, tn), BF16)],
        compiler_params=_cparams(2),
        name="gated_merge",
    )(y_a, y_b, y_c, w_pa, w_pb, w_pc, proj, proj, proj)


def _outproj_body(m_ref, w_ref, x_ref, o_ref, w_s):
    @pl.when(pl.program_id(1) == 0)
    def _():
        w_s[...] = w_ref[...].astype(BF16)

    o_ref[...] = x_ref[...] + jnp.dot(m_ref[...], w_s[...], preferred_element_type=F32)


def _out_proj(merged, w_out, layer, x):
    S, D = x.shape
    tm = min(512, S)
    tn = 1024
    return pl.pallas_call(
        _outproj_body,
        grid=(D // tn, S // tm),
        in_specs=[pl.BlockSpec((tm, D), lambda j, i: (i, 0)),
                  pl.BlockSpec((None, D, tn), lambda j, i: (layer, 0, j), pipeline_mode=pl.Buffered(1)),
                  pl.BlockSpec((tm, tn), lambda j, i: (i, j))],
        out_specs=pl.BlockSpec((tm, tn), lambda j, i: (i, j)),
        out_shape=jax.ShapeDtypeStruct((S, D), F32),
        scratch_shapes=[pltpu.VMEM((D, tn), BF16)],
        compiler_params=_cparams(2),
        name="out_proj",
    )(merged, w_out, x)


def _pack_w_uq(w_uq):
    w = w_uq.reshape(Q_RANK, MLA_HEADS, MLA_NOPE + MLA_ROPE).astype(BF16)
    nope = w[:, :, :MLA_NOPE].reshape(Q_RANK, MLA_HEADS * MLA_NOPE)
    rope = w[:, :, MLA_NOPE:].reshape(Q_RANK, MLA_HEADS * MLA_ROPE)
    return jnp.concatenate([nope, rope], axis=1)


def _rope_tables(S):
    half = MLA_ROPE // 2
    inv_freq = ROPE_THETA ** (-jnp.arange(0, MLA_ROPE, 2, dtype=F32) / MLA_ROPE)
    ang = jnp.arange(S, dtype=F32)[:, None] * inv_freq[None, :]
    cos, sin = jnp.cos(ang), jnp.sin(ang)
    z = lambda n: jnp.zeros((S, n), F32)
    cosf = jnp.concatenate([cos, cos, z(LANES - MLA_ROPE)], axis=1)
    sin_lo = jnp.concatenate([-sin, z(LANES - half)], axis=1)
    sin_hi = jnp.concatenate([z(half), sin, z(LANES - MLA_ROPE)], axis=1)
    return cosf, sin_lo, sin_hi


def _layer(x, tables, layer, norm_g, w_in_t, mla_gq, mla_gkv, w_uq_p, w_ukv_b, sg_ln_g, sg_ln_b, sg_ws, sg_bs,
           ml_conv_w, ml_conv_b, ml_wq_b, ml_wk_b, ml_bi, ml_bf, ml_gn_g, w_pa, w_pb, w_pc, w_out):
    h = _rmsnorm(x, norm_g, BF16)
    proj = _in_proj(h, w_in_t, layer)
    q = _q_proj(proj, mla_gq, w_uq_p, *tables)
    k, v = _kv_proj(proj, mla_gkv, w_ukv_b, *tables)
    y_a = _attention(q, k, v, proj)
    y_b = _spatial_gating(proj, sg_ln_g, sg_ln_b, sg_ws, sg_bs)
    y_c = _mlstm(proj, ml_conv_w, ml_conv_b, ml_wq_b, ml_wk_b, ml_bi, ml_bf, ml_gn_g)
    merged = _merge(y_a, y_b, y_c, w_pa, w_pb, w_pc, layer, proj)
    return _out_proj(merged, w_out, layer, x)


def kernel(x, norm_g, w_in, mla_gq, mla_gkv, mla_wuq, mla_wukv, sg_ln_g, sg_ln_b, sg_ws, sg_bs, ml_conv_w, ml_conv_b, ml_wq, ml_wk, ml_bi, ml_bf, ml_gn_g, w_pa, w_pb, w_pc, w_out, final_g):
    B, S, D = x.shape
    depth = w_in.shape[0]
    tables = _rope_tables(S)
    cast = lambda w: w.astype(BF16)
    w_in_t = jnp.swapaxes(w_in, 1, 2)
    outs = []
    for b in range(B):
        xb = x[b]
        for l in range(depth):
            xb = _layer(xb, tables, l, norm_g[l], w_in_t, mla_gq[l], mla_gkv[l],
                        _pack_w_uq(mla_wuq[l]), cast(mla_wukv[l]),
                        sg_ln_g[l], sg_ln_b[l], sg_ws[l], sg_bs[l], ml_conv_w[l], ml_conv_b[l],
                        cast(ml_wq[l]), cast(ml_wk[l]), ml_bi[l], ml_bf[l], ml_gn_g[l],
                        w_pa, w_pb, w_pc, w_out)
        outs.append(_rmsnorm(xb, final_g, x.dtype))
    return outs[0][None] if B == 1 else jnp.stack(outs, axis=0)
```

```python
import functools

import jax
import jax.numpy as jnp
from jax import lax
from jax.experimental import pallas as pl
from jax.experimental.pallas import tpu as pltpu

F32 = jnp.float32
BF16 = jnp.bfloat16

D_MODEL = 4096
CHUNK = 64
EPS = 1e-6

MLA_HEADS = 16
MLA_NOPE = 128
MLA_ROPE = 64
MLA_V = 128
Q_RANK = 1024
KV_RANK = 512
ROPE_THETA = 10000.0
MLA_WIDTH = MLA_HEADS * MLA_V
MLA_QK_PAD = 256

SG_BLOCK = 128
SG_GROUPS = 4
SG_WIDTH = 1024
SG_GROUP_DIM = SG_WIDTH // SG_GROUPS

ML_HEADS = 4
ML_WIDTH = 1024
ML_HEAD_DIM = 256
CONV_K = 4
ML_CHUNK = 128

LANES = 128
NEG = -1e30

COL_CQ = 0
COL_ZA = 1024
COL_SGU = 3072
COL_SGV = 4096
COL_ZB = 5120
COL_MLX = 6144
COL_MLV = 7168
COL_MLO = 8192
COL_ZC = 9216
COL_GATE = 10240
COL_CKV = 22528
COL_KR = 23040
COL_MLI = 23168
IN_PAD = 23552

VMEM_LIMIT = 56 * 1024 * 1024


def _cparams(n_axes, vmem=VMEM_LIMIT):
    return pltpu.CompilerParams(dimension_semantics=("arbitrary",) * n_axes,
                                vmem_limit_bytes=vmem)


def _sigmoid(x):
    return 1.0 / (1.0 + jnp.exp(-x))


def _silu(x):
    return x * _sigmoid(x)


def _rmsnorm_body(x_ref, g_ref, o_ref):
    x = x_ref[...]
    y = x * lax.rsqrt(jnp.mean(x * x, axis=-1, keepdims=True) + EPS)
    o_ref[...] = (y * g_ref[...]).astype(o_ref.dtype)


def _rmsnorm(x, g, out_dtype):
    S, D = x.shape
    tm = min(512, S)
    return pl.pallas_call(
        _rmsnorm_body,
        grid=(S // tm,),
        in_specs=[pl.BlockSpec((tm, D), lambda i: (i, 0)),
                  pl.BlockSpec((1, D), lambda i: (0, 0))],
        out_specs=pl.BlockSpec((tm, D), lambda i: (i, 0)),
        out_shape=jax.ShapeDtypeStruct((S, D), out_dtype),
        compiler_params=_cparams(1),
        name="rmsnorm",
    )(x, g.reshape(1, D))


IN_TN = 1024
IN_CHUNK = 128
IN_CHUNKS_PER_TILE = IN_TN // IN_CHUNK

_IN_TILE_SRC = (0, 1600, 2624, 3648, 4672, 5696, 6720, 7744, 8768, 9792) + tuple(
    10824 + IN_TN * t for t in range(3 * D_MODEL // IN_TN))
_IN_CHUNK_SRC = tuple(src + IN_CHUNK * c for src in _IN_TILE_SRC for c in range(IN_CHUNKS_PER_TILE)) + (
    1024, 1152, 1280, 1408, 1536, 10816, 0, 0)
assert len(_IN_CHUNK_SRC) * IN_CHUNK == IN_PAD


def _in_proj_body(src_ref, h_ref, wt_hbm, o_ref, stage, wb, sem, *, layer, chunks_per_step):
    j = pl.program_id(0)
    i = pl.program_id(1)
    slot = lax.rem(j, 2)

    def chunk_copy(tile, c, buf):
        row = pl.multiple_of(src_ref[tile * IN_CHUNKS_PER_TILE + c], 8)
        return pltpu.make_async_copy(wt_hbm.at[layer, pl.ds(row, IN_CHUNK), :], stage.at[buf], sem.at[buf])

    def start_fetch(tile, chunks):
        chunk_copy(tile, chunks[0], 0).start()

    def finish_fetch(tile, dst_slot, chunks):
        for n, c in enumerate(chunks):
            if n + 1 < len(chunks):
                chunk_copy(tile, chunks[n + 1], (n + 1) % 2).start()
            chunk_copy(tile, c, n % 2).wait()
            start = c * IN_CHUNK if isinstance(c, int) else pl.multiple_of(c * IN_CHUNK, IN_CHUNK)
            wb[dst_slot, pl.ds(start, IN_CHUNK), :] = stage[n % 2].astype(BF16)

    @pl.when(jnp.logical_and(j == 0, i == 0))
    def _():
        first_tile = list(range(IN_CHUNKS_PER_TILE))
        start_fetch(0, first_tile)
        finish_fetch(0, 0, first_tile)

    prefetch = jnp.logical_and(j + 1 < pl.num_programs(0), i * chunks_per_step < IN_CHUNKS_PER_TILE)
    share = [i * chunks_per_step + n for n in range(chunks_per_step)]

    @pl.when(prefetch)
    def _():
        start_fetch(j + 1, share)

    o_ref[...] = lax.dot_general(h_ref[...], wb[slot], (((1,), (1,)), ((), ())), preferred_element_type=F32)

    @pl.when(prefetch)
    def _():
        finish_fetch(j + 1, 1 - slot, share)


def _in_proj(h, w_in_t, layer):
    S, K = h.shape
    tm = min(1024, S)
    n_row_tiles = S // tm
    chunks_per_step = -(-IN_CHUNKS_PER_TILE // n_row_tiles)
    assert IN_CHUNKS_PER_TILE % chunks_per_step == 0
    grid_spec = pltpu.PrefetchScalarGridSpec(
        num_scalar_prefetch=1,
        grid=(IN_PAD // IN_TN, n_row_tiles),
        in_specs=[pl.BlockSpec((tm, K), lambda j, i, src: (i, 0)),
                  pl.BlockSpec(memory_space=pl.ANY)],
        out_specs=pl.BlockSpec((tm, IN_TN), lambda j, i, src: (i, j)),
        scratch_shapes=[pltpu.VMEM((2, IN_CHUNK, K), F32), pltpu.VMEM((2, IN_TN, K), BF16),
                        pltpu.SemaphoreType.DMA((2,))])
    return pl.pallas_call(
        functools.partial(_in_proj_body, layer=layer, chunks_per_step=chunks_per_step),
        grid_spec=grid_spec,
        out_shape=jax.ShapeDtypeStruct((S, IN_PAD), F32),
        compiler_params=_cparams(2),
        name="in_proj",
    )(jnp.asarray(_IN_CHUNK_SRC, jnp.int32), h, w_in_t)


PROJ_ROW_PARTS = 2


def _rope(seg, cosf, sin_lo, sin_hi):
    return (seg * cosf
            + pltpu.roll(seg, LANES - MLA_ROPE // 2, 1) * sin_lo
            + pltpu.roll(seg, MLA_ROPE // 2, 1) * sin_hi)


def _qproj_body(cq_ref, g_ref, wn_ref, wr_ref, cos_ref, slo_ref, shi_ref, o_ref, a_scr):
    @pl.when(pl.program_id(1) == 0)
    def _():
        c = cq_ref[...]
        y = c * lax.rsqrt(jnp.mean(c * c, axis=-1, keepdims=True) + EPS)
        a_scr[...] = (y * g_ref[...]).astype(BF16)

    part = a_scr.shape[0] // PROJ_ROW_PARTS
    for r in range(PROJ_ROW_PARTS):
        rows = slice(r * part, (r + 1) * part)
        nope = jnp.dot(a_scr[rows, :], wn_ref[...], preferred_element_type=F32) * ATTN_EXP2_SCALE
        rope = jnp.dot(a_scr[rows, :], wr_ref[...], preferred_element_type=F32) * ATTN_EXP2_SCALE
        cosf, slo, shi = cos_ref[rows, :], slo_ref[rows, :], shi_ref[rows, :]
        for hh in range(o_ref.shape[0]):
            o_ref[hh, rows, 0:MLA_NOPE] = nope[:, hh * MLA_NOPE:(hh + 1) * MLA_NOPE].astype(BF16)
            pair = rope[:, (hh // 2) * LANES:(hh // 2 + 1) * LANES]
            seg = pair if hh % 2 == 0 else pltpu.roll(pair, MLA_ROPE, 1)
            o_ref[hh, rows, MLA_NOPE:MLA_QK_PAD] = _rope(seg, cosf, slo, shi).astype(BF16)


def _q_proj(proj, g, w, cosf, slo, shi):
    S = proj.shape[0]
    tm = min(1024, S)
    hpb = 8
    rope0 = MLA_HEADS * MLA_NOPE // (hpb * MLA_ROPE)
    row = lambda i, j: (i, 0)
    return pl.pallas_call(
        _qproj_body,
        grid=(S // tm, MLA_HEADS // hpb),
        in_specs=[pl.BlockSpec((tm, Q_RANK), lambda i, j: (i, COL_CQ // Q_RANK)),
                  pl.BlockSpec((1, Q_RANK), lambda i, j: (0, 0)),
                  pl.BlockSpec((Q_RANK, hpb * MLA_NOPE), lambda i, j: (0, j)),
                  pl.BlockSpec((Q_RANK, hpb * MLA_ROPE), lambda i, j: (0, rope0 + j)),
                  pl.BlockSpec((tm, LANES), row),
                  pl.BlockSpec((tm, LANES), row),
                  pl.BlockSpec((tm, LANES), row)],
        out_specs=pl.BlockSpec((hpb, tm, MLA_QK_PAD), lambda i, j: (j, i, 0)),
        out_shape=jax.ShapeDtypeStruct((MLA_HEADS, S, MLA_QK_PAD), BF16),
        scratch_shapes=[pltpu.VMEM((tm, Q_RANK), BF16)],
        compiler_params=_cparams(2),
        name="mla_q_proj",
    )(proj, g.reshape(1, Q_RANK), w, w, cosf, slo, shi)


def _kvproj_body(ckv_ref, kr_ref, g_ref, w_ref, cos_ref, slo_ref, shi_ref, k_ref, v_ref, a_scr):
    @pl.when(pl.program_id(1) == 0)
    def _():
        c = ckv_ref[...]
        y = c * lax.rsqrt(jnp.mean(c * c, axis=-1, keepdims=True) + EPS)
        a_scr[...] = (y * g_ref[...]).astype(BF16)

    width = MLA_NOPE + MLA_V
    part = a_scr.shape[0] // PROJ_ROW_PARTS
    for r in range(PROJ_ROW_PARTS):
        rows = slice(r * part, (r + 1) * part)
        res = jnp.dot(a_scr[rows, :], w_ref[...], preferred_element_type=F32)
        k_rot = _rope(kr_ref[rows, :], cos_ref[rows, :], slo_ref[rows, :], shi_ref[rows, :]).astype(BF16)
        for hh in range(k_ref.shape[0]):
            base = hh * width
            k_ref[hh, rows, 0:MLA_NOPE] = res[:, base:base + MLA_NOPE].astype(BF16)
            k_ref[hh, rows, MLA_NOPE:MLA_QK_PAD] = k_rot
            v_ref[hh, rows, :] = res[:, base + MLA_NOPE:base + width].astype(BF16)


def _kv_proj(proj, g, w, cosf, slo, shi):
    S = proj.shape[0]
    tm = min(1024, S)
    hpb = 8
    tn = hpb * (MLA_NOPE + MLA_V)
    row = lambda i, j: (i, 0)
    return pl.pallas_call(
        _kvproj_body,
        grid=(S // tm, MLA_HEADS // hpb),
        in_specs=[pl.BlockSpec((tm, KV_RANK), lambda i, j: (i, COL_CKV // KV_RANK)),
                  pl.BlockSpec((tm, LANES), lambda i, j: (i, COL_KR // LANES)),
                  pl.BlockSpec((1, KV_RANK), lambda i, j: (0, 0)),
                  pl.BlockSpec((KV_RANK, tn), lambda i, j: (0, j)),
                  pl.BlockSpec((tm, LANES), row),
                  pl.BlockSpec((tm, LANES), row),
                  pl.BlockSpec((tm, LANES), row)],
        out_specs=[pl.BlockSpec((hpb, tm, MLA_QK_PAD), lambda i, j: (j, i, 0)),
                   pl.BlockSpec((hpb, tm, MLA_V), lambda i, j: (j, i, 0))],
        out_shape=[jax.ShapeDtypeStruct((MLA_HEADS, S, MLA_QK_PAD), BF16),
                   jax.ShapeDtypeStruct((MLA_HEADS, S, MLA_V), BF16)],
        scratch_shapes=[pltpu.VMEM((tm, KV_RANK), BF16)],
        compiler_params=_cparams(2),
        name="mla_kv_proj",
    )(proj, proj, g.reshape(1, KV_RANK), w, cosf, slo, shi)


ATTN_HEADS_PER_STEP = 2
ATTN_EXP2_SCALE = float(MLA_NOPE + MLA_ROPE) ** -0.5 * 1.4426950408889634
ATTN_STRIP = 64
ATTN_LOG2_STEPS_PER_TRIP = 3


def _attn_body(q_ref, k_ref, v_ref, z_ref, o_ref, s_scr, p_scr, m_scr, a_scr, acc_scr, *, blk):
    i = pl.program_id(1)
    A, B = 0, 1
    ones_col = (lax.broadcasted_iota(jnp.int32, (blk, MLA_V), 1) == 0).astype(BF16)
    col_chunk = lax.broadcasted_iota(jnp.int32, (ATTN_STRIP, blk), 1) // CHUNK
    row_in_strip = lax.broadcasted_iota(jnp.int32, (ATTN_STRIP, blk), 0)

    def score(h, kb):
        k = k_ref[h, pl.ds(pl.multiple_of(kb * blk, blk), blk), :]
        s_scr[h] = lax.dot_general(q_ref[h], k, (((1,), (1,)), ((), ())), preferred_element_type=F32)

    def softmax(h, diagonal):
        for r in range(blk // ATTN_STRIP):
            rows = slice(r * ATTN_STRIP, (r + 1) * ATTN_STRIP)
            s = s_scr[h, rows, :]
            if diagonal:
                s = jnp.where(col_chunk <= (row_in_strip + r * ATTN_STRIP) // CHUNK, s, NEG)
                m_new = jnp.broadcast_to(jnp.max(s, axis=-1, keepdims=True), (ATTN_STRIP, LANES))
                a_scr[h, rows, :] = jnp.ones((ATTN_STRIP, LANES), F32)
            else:
                m_old = m_scr[h, rows, :]
                m_new = jnp.maximum(m_old, jnp.max(s, axis=-1, keepdims=True))
                a_scr[h, rows, :] = jnp.exp2(m_old - m_new)
            m_scr[h, rows, :] = m_new
            p_scr[h, rows, :] = jnp.exp2(s - jnp.concatenate([m_new] * (blk // LANES), axis=1)).astype(BF16)

    def apply_values(h, kb):
        v = v_ref[h, pl.ds(pl.multiple_of(kb * blk, blk), blk), :]
        pv = jnp.dot(p_scr[h], jnp.concatenate([v, ones_col], axis=1), preferred_element_type=F32)
        a = a_scr[h]
        acc_scr[h] = jnp.concatenate([a, a], axis=1) * acc_scr[h] + pv

    acc_scr[...] = jnp.zeros_like(acc_scr)
    score(A, i)
    softmax(A, True)
    score(B, i)
    softmax(B, True)
    score(A, 0)
    apply_values(A, i)

    def step(j):
        softmax(A, False)
        score(B, j - 1)
        apply_values(B, jnp.where(j == 1, i, j - 2))
        softmax(B, False)
        score(A, j)
        apply_values(A, j - 1)

    full_trips = lax.shift_right_logical(i, ATTN_LOG2_STEPS_PER_TRIP)
    per_trip = 1 << ATTN_LOG2_STEPS_PER_TRIP

    def trip(t, carry):
        for n in range(per_trip):
            step(per_trip * t + 1 + n)
        return carry

    lax.fori_loop(0, full_trips, trip, 0)

    done = per_trip * full_trips
    for bit in reversed(range(ATTN_LOG2_STEPS_PER_TRIP)):
        group = 1 << bit
        has_group = lax.bitwise_and(i, group) != 0

        @pl.when(has_group)
        def _(done=done, group=group):
            for n in range(group):
                step(done + 1 + n)

        done = done + jnp.where(has_group, group, 0)

    apply_values(B, jnp.maximum(i - 1, 0))
    for h in (A, B):
        cols = slice(h * MLA_V, (h + 1) * MLA_V)
        attn = acc_scr[h, :, 0:MLA_V] / acc_scr[h, :, MLA_V:MLA_V + 1]
        o_ref[:, cols] = (attn * _silu(z_ref[:, cols])).astype(o_ref.dtype)


def _attention(q, k, v, proj):
    H, S, _ = q.shape
    blk = min(512, S)
    hp = ATTN_HEADS_PER_STEP
    assert hp == 2
    return pl.pallas_call(
        functools.partial(_attn_body, blk=blk),
        grid=(H // hp, S // blk),
        in_specs=[pl.BlockSpec((hp, blk, MLA_QK_PAD), lambda h, i: (h, i, 0)),
                  pl.BlockSpec((hp, S, MLA_QK_PAD), lambda h, i: (h, 0, 0)),
                  pl.BlockSpec((hp, S, MLA_V), lambda h, i: (h, 0, 0)),
                  pl.BlockSpec((blk, hp * MLA_V), lambda h, i: (i, COL_ZA // (hp * MLA_V) + h))],
        out_specs=pl.BlockSpec((blk, hp * MLA_V), lambda h, i: (i, h)),
        out_shape=jax.ShapeDtypeStruct((S, MLA_WIDTH), BF16),
        scratch_shapes=[pltpu.VMEM((hp, blk, blk), F32), pltpu.VMEM((hp, blk, blk), BF16),
                        pltpu.VMEM((hp, blk, LANES), F32), pltpu.VMEM((hp, blk, LANES), F32),
                        pltpu.VMEM((hp, blk, 2 * MLA_V), F32)],
        compiler_params=_cparams(2),
        name="mla_attention",
    )(q, k, v, proj)


def _sgu_body(u_ref, v_ref, z_ref, lng_ref, lnb_ref, ws_ref, bst_ref, o_ref):
    v = v_ref[...]
    mu = jnp.mean(v, axis=-1, keepdims=True)
    vc = v - mu
    var = jnp.mean(vc * vc, axis=-1, keepdims=True)
    vn = (vc * lax.rsqrt(var + EPS) * lng_ref[...] + lnb_ref[...]).astype(BF16)
    out_chunk = lax.broadcasted_iota(jnp.int32, (SG_BLOCK, SG_BLOCK), 0) // CHUNK
    in_chunk = lax.broadcasted_iota(jnp.int32, (SG_BLOCK, SG_BLOCK), 1) // CHUNK
    causal = in_chunk <= out_chunk
    bst = bst_ref[...]
    for g in range(SG_GROUPS):
        w = jnp.where(causal, ws_ref[g], 0.0).astype(BF16)
        cols = slice(g * SG_GROUP_DIM, (g + 1) * SG_GROUP_DIM)
        bias = bst[:, g:g + 1]
        for b in range(u_ref.shape[0] // SG_BLOCK):
            rows = slice(b * SG_BLOCK, (b + 1) * SG_BLOCK)
            s = jnp.dot(w, vn[rows, cols], preferred_element_type=F32) + bias
            o_ref[rows, cols] = (u_ref[rows, cols] * s * _silu(z_ref[rows, cols])).astype(o_ref.dtype)


def _spatial_gating(proj, ln_g, ln_b, w_s, b_s):
    S = proj.shape[0]
    tm = min(512, S)
    blk = lambda c: pl.BlockSpec((tm, SG_WIDTH), lambda i: (i, c // SG_WIDTH))
    full = lambda shape: pl.BlockSpec(shape, lambda i: (0,) * len(shape))
    return pl.pallas_call(
        _sgu_body,
        grid=(S // tm,),
        in_specs=[blk(COL_SGU), blk(COL_SGV), blk(COL_ZB),
                  full((1, SG_WIDTH)), full((1, SG_WIDTH)),
                  full((SG_GROUPS, SG_BLOCK, SG_BLOCK)), full((SG_BLOCK, SG_GROUPS))],
        out_specs=pl.BlockSpec((tm, SG_WIDTH), lambda i: (i, 0)),
        out_shape=jax.ShapeDtypeStruct((S, SG_WIDTH), BF16),
        compiler_params=_cparams(1),
        name="spatial_gating",
    )(proj, proj, proj, ln_g.reshape(1, SG_WIDTH), ln_b.reshape(1, SG_WIDTH), w_s, jnp.transpose(b_s))


def _mlstm_body(x_ref, v_ref, og_ref, z_ref, if_ref, cw_ref, cb_ref, wq_ref, wk_ref,
                bi_ref, bf_ref, gn_ref, o_ref, xs_scr, ct_scr, n_scr, m_scr, h_scr, *, T, L):
    halo = 8

    @pl.when(pl.program_id(0) == 0)
    def _():
        xs_scr[0:halo, :] = jnp.zeros((halo, ML_WIDTH), F32)
        ct_scr[...] = jnp.zeros_like(ct_scr)
        n_scr[...] = jnp.zeros_like(n_scr)
        m_scr[...] = jnp.zeros_like(m_scr)

    xs_scr[halo:halo + T, :] = x_ref[...]
    cw = cw_ref[...]
    xc = cb_ref[...]
    for kk in range(CONV_K):
        off = halo - (CONV_K - 1) + kk
        xc = xc + cw[kk:kk + 1, :] * xs_scr[off:off + T, :]
    xs_scr[0:halo, :] = xs_scr[T:T + halo, :]
    xcb = _silu(xc).astype(BF16)

    gates = if_ref[...]
    head_lane = lax.broadcasted_iota(jnp.int32, gates.shape, 1) < ML_HEADS
    ig = jnp.where(head_lane, gates + bi_ref[...], 0.0)
    fg = jnp.where(head_lane, pltpu.roll(gates, LANES - ML_HEADS, 1) + bf_ref[...], 0.0)
    lf = jnp.minimum(fg, 0.0) - jnp.log1p(jnp.exp(-jnp.abs(fg)))
    rr = lax.broadcasted_iota(jnp.int32, (L, L), 0)
    cc = lax.broadcasted_iota(jnp.int32, (L, L), 1)
    tril = cc <= rr
    tri_f = tril.astype(F32)
    nc = T // L
    g_l, G_l, a_l, e_l, r_l = [], [], [], [], []
    for c in range(nc):
        rows = slice(c * L, (c + 1) * L)
        g_c = jnp.dot(tri_f, lf[rows], precision=lax.Precision.HIGHEST, preferred_element_type=F32)
        G_c = g_c[L - 1:L, :]
        w_end = G_c - g_c + ig[rows]
        a_c = jnp.max(w_end, axis=0, keepdims=True)
        g_l.append(g_c)
        G_l.append(G_c)
        a_l.append(a_c)
        e_l.append(jnp.exp(w_end - a_c))
        r_l.append(ig[rows] - g_c)
    r_t = jnp.transpose(jnp.concatenate(r_l, axis=0))

    q_l, k_l = [], []
    for h in range(ML_HEADS):
        cols = slice(h * ML_HEAD_DIM, (h + 1) * ML_HEAD_DIM)
        q_l.append(jnp.dot(xcb[:, cols], wq_ref[h], preferred_element_type=F32))
        k_l.append(jnp.dot(xcb[:, cols], wk_ref[h], preferred_element_type=F32) * (ML_HEAD_DIM ** -0.5))

    m_vec = m_scr[0:1, :]
    for c in range(nc):
        rows = slice(c * L, (c + 1) * L)
        g_c, G_c, a_c, e_c = g_l[c], G_l[c], a_l[c], e_l[c]
        m_new = jnp.maximum(G_c + m_vec, a_c)
        sp = jnp.exp(G_c + m_vec - m_new)
        sl = jnp.exp(a_c - m_new)
        for h in range(ML_HEADS):
            cols = slice(h * ML_HEAD_DIM, (h + 1) * ML_HEAD_DIM)
            q_c = q_l[h][rows]
            k_c = k_l[h][rows]
            v_c = v_ref[rows, cols]
            q_cb = q_c.astype(BF16)
            g_col = g_c[:, h:h + 1]
            d = jnp.where(tril, g_col + r_t[h:h + 1, rows], NEG)
            inter_log = g_col + m_vec[:, h:h + 1]
            m_i = jnp.maximum(jnp.max(d, axis=1, keepdims=True), inter_log)
            p = jnp.exp(d - m_i)
            qk = lax.dot_general(q_cb, k_c.astype(BF16), (((1,), (1,)), ((), ())),
                                 preferred_element_type=F32)
            sm = qk * p
            inter_scale = jnp.exp(inter_log - m_i)
            ct = ct_scr[h]
            n_prev = n_scr[h:h + 1, :]
            num = (jnp.dot(sm.astype(BF16), v_c.astype(BF16), preferred_element_type=F32)
                   + inter_scale * jnp.dot(q_cb, ct.astype(BF16), preferred_element_type=F32))
            den = (jnp.sum(sm, axis=1, keepdims=True)
                   + inter_scale * jnp.sum(q_c * n_prev, axis=1, keepdims=True))
            h_scr[rows, cols] = num / jnp.maximum(jnp.abs(den), jnp.exp(-m_i))

            e_col = e_c[:, h:h + 1]
            ev = (e_col * v_c).astype(BF16)
            c_loc_t = jnp.dot(jnp.transpose(k_c).astype(BF16), ev, preferred_element_type=F32)
            sp_h = sp[:, h:h + 1]
            sl_h = sl[:, h:h + 1]
            ct_scr[h] = sp_h * ct + sl_h * c_loc_t
            n_scr[h:h + 1, :] = sp_h * n_prev + sl_h * jnp.sum(e_col * k_c, axis=0, keepdims=True)
        m_vec = m_new
    m_scr[0:1, :] = m_vec

    for h in range(ML_HEADS):
        cols = slice(h * ML_HEAD_DIM, (h + 1) * ML_HEAD_DIM)
        hs = h_scr[:, cols] * _sigmoid(og_ref[:, cols])
        mu = jnp.mean(hs, axis=-1, keepdims=True)
        hc = hs - mu
        var = jnp.mean(hc * hc, axis=-1, keepdims=True)
        y = hc * lax.rsqrt(var + EPS) * gn_ref[:, cols]
        o_ref[:, cols] = (y * _silu(z_ref[:, cols])).astype(o_ref.dtype)


def _mlstm(proj, conv_w, conv_b, w_q, w_k, b_i, b_f, gn_g):
    S = proj.shape[0]
    T = min(512, S)
    L = ML_CHUNK
    wide = lambda c: pl.BlockSpec((T, ML_WIDTH), lambda i: (i, c // ML_WIDTH))
    gate = lambda c: pl.BlockSpec((T, LANES), lambda i: (i, c // LANES))
    full = lambda shape: pl.BlockSpec(shape, lambda i: (0,) * len(shape))
    pad_gate_bias = lambda b: jnp.pad(b, (0, LANES - ML_HEADS)).reshape(1, LANES)
    return pl.pallas_call(
        functools.partial(_mlstm_body, T=T, L=L),
        grid=(S // T,),
        in_specs=[wide(COL_MLX), wide(COL_MLV), wide(COL_MLO), wide(COL_ZC),
                  gate(COL_MLI),
                  full((CONV_K, ML_WIDTH)), full((1, ML_WIDTH)),
                  full((ML_HEADS, ML_HEAD_DIM, ML_HEAD_DIM)), full((ML_HEADS, ML_HEAD_DIM, ML_HEAD_DIM)),
                  full((1, LANES)), full((1, LANES)), full((1, ML_WIDTH))],
        out_specs=pl.BlockSpec((T, ML_WIDTH), lambda i: (i, 0)),
        out_shape=jax.ShapeDtypeStruct((S, ML_WIDTH), BF16),
        scratch_shapes=[pltpu.VMEM((T + 8, ML_WIDTH), F32),
                        pltpu.VMEM((ML_HEADS, ML_HEAD_DIM, ML_HEAD_DIM), F32),
                        pltpu.VMEM((8, ML_HEAD_DIM), F32),
                        pltpu.VMEM((8, LANES), F32),
                        pltpu.VMEM((T, ML_WIDTH), F32)],
        compiler_params=_cparams(1),
        name="mlstm",
    )(proj, proj, proj, proj, proj, conv_w, conv_b.reshape(1, ML_WIDTH), w_q, w_k,
      pad_gate_bias(b_i), pad_gate_bias(b_f), gn_g.reshape(1, ML_WIDTH))


def _merge_body(ya_ref, yb_ref, yc_ref, wa_ref, wb_ref, wc_ref, g0_ref, g1_ref, g2_ref, o_ref, wa_s, wb_s, wc_s):
    @pl.when(pl.program_id(1) == 0)
    def _():
        wa_s[...] = wa_ref[...].astype(BF16)
        wb_s[...] = wb_ref[...].astype(BF16)
        wc_s[...] = wc_ref[...].astype(BF16)

    pa = jnp.dot(ya_ref[...], wa_s[...], preferred_element_type=F32)
    pb = jnp.dot(yb_ref[...], wb_s[...], preferred_element_type=F32)
    pc = jnp.dot(yc_ref[...], wc_s[...], preferred_element_type=F32)
    merged = _sigmoid(g0_ref[...]) * pa + _sigmoid(g1_ref[...]) * pb + _sigmoid(g2_ref[...]) * pc
    o_ref[...] = merged.astype(o_ref.dtype)


def _merge(y_a, y_b, y_c, w_pa, w_pb, w_pc, layer, proj):
    S = y_a.shape[0]
    D = w_pa.shape[-1]
    tm = min(512, S)
    tn = 1024
    act = lambda width: pl.BlockSpec((tm, width), lambda j, i: (i, 0))
    wgt = lambda width: pl.BlockSpec((None, width, tn), lambda j, i: (layer, 0, j), pipeline_mode=pl.Buffered(1))
    gate = lambda b: pl.BlockSpec((tm, tn), lambda j, i: (i, (COL_GATE + b * D) // tn + j))
    return pl.pallas_call(
        _merge_body,
        grid=(D // tn, S // tm),
        in_specs=[act(MLA_WIDTH), act(SG_WIDTH), act(ML_WIDTH),
                  wgt(MLA_WIDTH), wgt(SG_WIDTH), wgt(ML_WIDTH),
                  gate(0), gate(1), gate(2)],
        out_specs=pl.BlockSpec((tm, tn), lambda j, i: (i, j)),
        out_shape=jax.ShapeDtypeStruct((S, D), BF16),
        scratch_shapes=[pltpu.VMEM((MLA_WIDTH, tn), BF16), pltpu.VMEM((SG_WIDTH, tn), BF16),
                        pltpu.VMEM((ML_WIDTH, tn), BF16)],
        compiler_params=_cparams(2),
        name="gated_merge",
    )(y_a, y_b, y_c, w_pa, w_pb, w_pc, proj, proj, proj)


def _outproj_body(m_ref, w_ref, x_ref, o_ref, w_s):
    @pl.when(pl.program_id(1) == 0)
    def _():
        w_s[...] = w_ref[...].astype(BF16)

    o_ref[...] = x_ref[...] + jnp.dot(m_ref[...], w_s[...], preferred_element_type=F32)


def _out_proj(merged, w_out, layer, x):
    S, D = x.shape
    tm = min(512, S)
    tn = 1024
    return pl.pallas_call(
        _outproj_body,
        grid=(D // tn, S // tm),
        in_specs=[pl.BlockSpec((tm, D), lambda j, i: (i, 0)),
                  pl.BlockSpec((None, D, tn), lambda j, i: (layer, 0, j), pipeline_mode=pl.Buffered(1)),
                  pl.BlockSpec((tm, tn), lambda j, i: (i, j))],
        out_specs=pl.BlockSpec((tm, tn), lambda j, i: (i, j)),
        out_shape=jax.ShapeDtypeStruct((S, D), F32),
        scratch_shapes=[pltpu.VMEM((D, tn), BF16)],
        compiler_params=_cparams(2),
        name="out_proj",
    )(merged, w_out, x)


def _pack_w_uq(w_uq):
    w = w_uq.reshape(Q_RANK, MLA_HEADS, MLA_NOPE + MLA_ROPE).astype(BF16)
    nope = w[:, :, :MLA_NOPE].reshape(Q_RANK, MLA_HEADS * MLA_NOPE)
    rope = w[:, :, MLA_NOPE:].reshape(Q_RANK, MLA_HEADS * MLA_ROPE)
    return jnp.concatenate([nope, rope], axis=1)


def _rope_tables(S):
    half = MLA_ROPE // 2
    inv_freq = ROPE_THETA ** (-jnp.arange(0, MLA_ROPE, 2, dtype=F32) / MLA_ROPE)
    ang = jnp.arange(S, dtype=F32)[:, None] * inv_freq[None, :]
    cos, sin = jnp.cos(ang), jnp.sin(ang)
    z = lambda n: jnp.zeros((S, n), F32)
    cosf = jnp.concatenate([cos, cos, z(LANES - MLA_ROPE)], axis=1)
    sin_lo = jnp.concatenate([-sin, z(LANES - half)], axis=1)
    sin_hi = jnp.concatenate([z(half), sin, z(LANES - MLA_ROPE)], axis=1)
    return cosf, sin_lo, sin_hi


def _layer(x, tables, layer, norm_g, w_in_t, mla_gq, mla_gkv, w_uq_p, w_ukv_b, sg_ln_g, sg_ln_b, sg_ws, sg_bs,
           ml_conv_w, ml_conv_b, ml_wq_b, ml_wk_b, ml_bi, ml_bf, ml_gn_g, w_pa, w_pb, w_pc, w_out):
    h = _rmsnorm(x, norm_g, BF16)
    proj = _in_proj(h, w_in_t, layer)
    q = _q_proj(proj, mla_gq, w_uq_p, *tables)
    k, v = _kv_proj(proj, mla_gkv, w_ukv_b, *tables)
    y_a = _attention(q, k, v, proj)
    y_b = _spatial_gating(proj, sg_ln_g, sg_ln_b, sg_ws, sg_bs)
    y_c = _mlstm(proj, ml_conv_w, ml_conv_b, ml_wq_b, ml_wk_b, ml_bi, ml_bf, ml_gn_g)
    merged = _merge(y_a, y_b, y_c, w_pa, w_pb, w_pc, layer, proj)
    return _out_proj(merged, w_out, layer, x)


def kernel(x, norm_g, w_in, mla_gq, mla_gkv, mla_wuq, mla_wukv, sg_ln_g, sg_ln_b, sg_ws, sg_bs, ml_conv_w, ml_conv_b, ml_wq, ml_wk, ml_bi, ml_bf, ml_gn_g, w_pa, w_pb, w_pc, w_out, final_g):
    B, S, D = x.shape
    depth = w_in.shape[0]
    tables = _rope_tables(S)
    cast = lambda w: w.astype(BF16)
    w_in_t = jnp.swapaxes(w_in, 1, 2)
    outs = []
    for b in range(B):
        xb = x[b]
        for l in range(depth):
            xb = _layer(xb, tables, l, norm_g[l], w_in_t, mla_gq[l], mla_gkv[l],
                        _pack_w_uq(mla_wuq[l]), cast(mla_wukv[l]),
                        sg_ln_g[l], sg_ln_b[l], sg_ws[l], sg_bs[l], ml_conv_w[l], ml_conv_b[l],
                        cast(ml_wq[l]), cast(ml_wk[l]), ml_bi[l], ml_bf[l], ml_gn_g[l],
                        w_pa, w_pb, w_pc, w_out)
        outs.append(_rmsnorm(xb, final_g, x.dtype))
    return outs[0][None] if B == 1 else jnp.stack(outs, axis=0)
```

```python
import functools

import jax
import jax.numpy as jnp
from jax import lax
from jax.experimental import pallas as pl
from jax.experimental.pallas import tpu as pltpu

F32 = jnp.float32
BF16 = jnp.bfloat16

D_MODEL = 4096
CHUNK = 64
EPS = 1e-6

MLA_HEADS = 16
MLA_NOPE = 128
MLA_ROPE = 64
MLA_V = 128
Q_RANK = 1024
KV_RANK = 512
ROPE_THETA = 10000.0
MLA_WIDTH = MLA_HEADS * MLA_V
MLA_QK_PAD = 256

SG_BLOCK = 128
SG_GROUPS = 4
SG_WIDTH = 1024
SG_GROUP_DIM = SG_WIDTH // SG_GROUPS

ML_HEADS = 4
ML_WIDTH = 1024
ML_HEAD_DIM = 256
CONV_K = 4
ML_CHUNK = 128

LANES = 128
NEG = -1e30

COL_CQ = 0
COL_ZA = 1024
COL_SGU = 3072
COL_SGV = 4096
COL_ZB = 5120
COL_MLX = 6144
COL_MLV = 7168
COL_MLO = 8192
COL_ZC = 9216
COL_GATE = 10240
COL_CKV = 22528
COL_KR = 23040
COL_MLI = 23168
IN_PAD = 23552

VMEM_LIMIT = 56 * 1024 * 1024


def _cparams(n_axes, vmem=VMEM_LIMIT):
    return pltpu.CompilerParams(dimension_semantics=("arbitrary",) * n_axes,
                                vmem_limit_bytes=vmem)


def _sigmoid(x):
    return 1.0 / (1.0 + jnp.exp(-x))


def _silu(x):
    return x * _sigmoid(x)


def _rmsnorm_body(x_ref, g_ref, o_ref):
    x = x_ref[...]
    y = x * lax.rsqrt(jnp.mean(x * x, axis=-1, keepdims=True) + EPS)
    o_ref[...] = (y * g_ref[...]).astype(o_ref.dtype)


def _rmsnorm(x, g, out_dtype):
    S, D = x.shape
    tm = min(512, S)
    return pl.pallas_call(
        _rmsnorm_body,
        grid=(S // tm,),
        in_specs=[pl.BlockSpec((tm, D), lambda i: (i, 0)),
                  pl.BlockSpec((1, D), lambda i: (0, 0))],
        out_specs=pl.BlockSpec((tm, D), lambda i: (i, 0)),
        out_shape=jax.ShapeDtypeStruct((S, D), out_dtype),
        compiler_params=_cparams(1),
        name="rmsnorm",
    )(x, g.reshape(1, D))


IN_TN = 1024
IN_CHUNK = 128
IN_CHUNKS_PER_TILE = IN_TN // IN_CHUNK

_IN_TILE_SRC = (0, 1600, 2624, 3648, 4672, 5696, 6720, 7744, 8768, 9792) + tuple(
    10824 + IN_TN * t for t in range(3 * D_MODEL // IN_TN))
_IN_CHUNK_SRC = tuple(src + IN_CHUNK * c for src in _IN_TILE_SRC for c in range(IN_CHUNKS_PER_TILE)) + (
    1024, 1152, 1280, 1408, 1536, 10816, 0, 0)
assert len(_IN_CHUNK_SRC) * IN_CHUNK == IN_PAD


def _in_proj_body(src_ref, h_ref, wt_hbm, o_ref, stage, wb_even, wb_odd, sem, *, layer, chunks_per_step):
    j = pl.program_id(0)
    i = pl.program_id(1)
    n_tiles = pl.num_programs(0)
    slot = lax.rem(j, 2)
    first_step = jnp.logical_and(j == 0, i == 0)
    last_step = jnp.logical_and(j == n_tiles - 1, i == pl.num_programs(1) - 1)

    def chunk_copy(tile, c):
        buf = c % 2 if isinstance(c, int) else lax.rem(c, 2)
        row = pl.multiple_of(src_ref[tile * IN_CHUNKS_PER_TILE + c], 8)
        return pltpu.make_async_copy(wt_hbm.at[layer, pl.ds(row, IN_CHUNK), :], stage.at[buf], sem.at[buf])

    def cast_chunk(dst_ref, c):
        buf = c % 2 if isinstance(c, int) else lax.rem(c, 2)
        start = c * IN_CHUNK if isinstance(c, int) else pl.multiple_of(c * IN_CHUNK, IN_CHUNK)
        dst_ref[pl.ds(start, IN_CHUNK), :] = stage[buf].astype(BF16)

    next_tile = jnp.minimum(j + 1, n_tiles - 1)

    @pl.when(first_step)
    def _():
        chunk_copy(0, 0).start()
        for c in range(IN_CHUNKS_PER_TILE):
            if c + 1 < IN_CHUNKS_PER_TILE:
                chunk_copy(0, c + 1).start()
            chunk_copy(0, c).wait()
            cast_chunk(wb_even, c)
        chunk_copy(next_tile, 0).start()

    def sweep_step(this_tile_ref, next_tile_ref):
        for n in range(chunks_per_step):
            c = i * chunks_per_step + n
            chunk_copy(next_tile, c).wait()
            if n + 1 < chunks_per_step:
                chunk_copy(next_tile, c + 1).start()
            else:
                wraps = c + 1 == IN_CHUNKS_PER_TILE
                following_tile = jnp.where(wraps, jnp.minimum(j + 2, n_tiles - 1), next_tile)
                following_chunk = jnp.where(wraps, 0, c + 1)

                @pl.when(jnp.logical_not(last_step))
                def _():
                    chunk_copy(following_tile, following_chunk).start()

            cast_chunk(next_tile_ref, c)

        o_ref[...] = lax.dot_general(h_ref[...], this_tile_ref[...], (((1,), (1,)), ((), ())),
                                     preferred_element_type=F32)

    pl.when(slot == 0)(functools.partial(sweep_step, wb_even, wb_odd))
    pl.when(slot == 1)(functools.partial(sweep_step, wb_odd, wb_even))


def _in_proj(h, w_in_t, layer):
    S, K = h.shape
    tm = min(1024, S)
    n_row_tiles = S // tm
    chunks_per_step = IN_CHUNKS_PER_TILE // n_row_tiles
    assert chunks_per_step * n_row_tiles == IN_CHUNKS_PER_TILE
    grid_spec = pltpu.PrefetchScalarGridSpec(
        num_scalar_prefetch=1,
        grid=(IN_PAD // IN_TN, n_row_tiles),
        in_specs=[pl.BlockSpec((tm, K), lambda j, i, src: (i, 0)),
                  pl.BlockSpec(memory_space=pl.ANY)],
        out_specs=pl.BlockSpec((tm, IN_TN), lambda j, i, src: (i, j)),
        scratch_shapes=[pltpu.VMEM((2, IN_CHUNK, K), F32), pltpu.VMEM((IN_TN, K), BF16), pltpu.VMEM((IN_TN, K), BF16),
                        pltpu.SemaphoreType.DMA((2,))])
    return pl.pallas_call(
        functools.partial(_in_proj_body, layer=layer, chunks_per_step=chunks_per_step),
        grid_spec=grid_spec,
        out_shape=jax.ShapeDtypeStruct((S, IN_PAD), F32),
        compiler_params=_cparams(2),
        name="in_proj",
    )(jnp.asarray(_IN_CHUNK_SRC, jnp.int32), h, w_in_t)


PROJ_ROW_PARTS = 2


def _rope(seg, cosf, sin_lo, sin_hi):
    return (seg * cosf
            + pltpu.roll(seg, LANES - MLA_ROPE // 2, 1) * sin_lo
            + pltpu.roll(seg, MLA_ROPE // 2, 1) * sin_hi)


def _qproj_body(cq_ref, g_ref, wn_ref, wr_ref, cos_ref, slo_ref, shi_ref, o_ref, a_scr):
    @pl.when(pl.program_id(1) == 0)
    def _():
        c = cq_ref[...]
        y = c * lax.rsqrt(jnp.mean(c * c, axis=-1, keepdims=True) + EPS)
        a_scr[...] = (y * g_ref[...]).astype(BF16)

    part = a_scr.shape[0] // PROJ_ROW_PARTS
    for r in range(PROJ_ROW_PARTS):
        rows = slice(r * part, (r + 1) * part)
        nope = jnp.dot(a_scr[rows, :], wn_ref[...], preferred_element_type=F32) * ATTN_EXP2_SCALE
        rope = jnp.dot(a_scr[rows, :], wr_ref[...], preferred_element_type=F32) * ATTN_EXP2_SCALE
        cosf, slo, shi = cos_ref[rows, :], slo_ref[rows, :], shi_ref[rows, :]
        for hh in range(o_ref.shape[0]):
            o_ref[hh, rows, 0:MLA_NOPE] = nope[:, hh * MLA_NOPE:(hh + 1) * MLA_NOPE].astype(BF16)
            pair = rope[:, (hh // 2) * LANES:(hh // 2 + 1) * LANES]
            seg = pair if hh % 2 == 0 else pltpu.roll(pair, MLA_ROPE, 1)
            o_ref[hh, rows, MLA_NOPE:MLA_QK_PAD] = _rope(seg, cosf, slo, shi).astype(BF16)


def _q_proj(proj, g, w, cosf, slo, shi):
    S = proj.shape[0]
    tm = min(1024, S)
    hpb = 8
    rope0 = MLA_HEADS * MLA_NOPE // (hpb * MLA_ROPE)
    row = lambda i, j: (i, 0)
    return pl.pallas_call(
        _qproj_body,
        grid=(S // tm, MLA_HEADS // hpb),
        in_specs=[pl.BlockSpec((tm, Q_RANK), lambda i, j: (i, COL_CQ // Q_RANK)),
                  pl.BlockSpec((1, Q_RANK), lambda i, j: (0, 0)),
                  pl.BlockSpec((Q_RANK, hpb * MLA_NOPE), lambda i, j: (0, j)),
                  pl.BlockSpec((Q_RANK, hpb * MLA_ROPE), lambda i, j: (0, rope0 + j)),
                  pl.BlockSpec((tm, LANES), row),
                  pl.BlockSpec((tm, LANES), row),
                  pl.BlockSpec((tm, LANES), row)],
        out_specs=pl.BlockSpec((hpb, tm, MLA_QK_PAD), lambda i, j: (j, i, 0)),
        out_shape=jax.ShapeDtypeStruct((MLA_HEADS, S, MLA_QK_PAD), BF16),
        scratch_shapes=[pltpu.VMEM((tm, Q_RANK), BF16)],
        compiler_params=_cparams(2),
        name="mla_q_proj",
    )(proj, g.reshape(1, Q_RANK), w, w, cosf, slo, shi)


def _kvproj_body(ckv_ref, kr_ref, g_ref, w_ref, cos_ref, slo_ref, shi_ref, k_ref, v_ref, a_scr):
    @pl.when(pl.program_id(1) == 0)
    def _():
        c = ckv_ref[...]
        y = c * lax.rsqrt(jnp.mean(c * c, axis=-1, keepdims=True) + EPS)
        a_scr[...] = (y * g_ref[...]).astype(BF16)

    width = MLA_NOPE + MLA_V
    part = a_scr.shape[0] // PROJ_ROW_PARTS
    for r in range(PROJ_ROW_PARTS):
        rows = slice(r * part, (r + 1) * part)
        res = jnp.dot(a_scr[rows, :], w_ref[...], preferred_element_type=F32)
        k_rot = _rope(kr_ref[rows, :], cos_ref[rows, :], slo_ref[rows, :], shi_ref[rows, :]).astype(BF16)
        for hh in range(k_ref.shape[0]):
            base = hh * width
            k_ref[hh, rows, 0:MLA_NOPE] = res[:, base:base + MLA_NOPE].astype(BF16)
            k_ref[hh, rows, MLA_NOPE:MLA_QK_PAD] = k_rot
            v_ref[hh, rows, :] = res[:, base + MLA_NOPE:base + width].astype(BF16)


def _kv_proj(proj, g, w, cosf, slo, shi):
    S = proj.shape[0]
    tm = min(1024, S)
    hpb = 8
    tn = hpb * (MLA_NOPE + MLA_V)
    row = lambda i, j: (i, 0)
    return pl.pallas_call(
        _kvproj_body,
        grid=(S // tm, MLA_HEADS // hpb),
        in_specs=[pl.BlockSpec((tm, KV_RANK), lambda i, j: (i, COL_CKV // KV_RANK)),
                  pl.BlockSpec((tm, LANES), lambda i, j: (i, COL_KR // LANES)),
                  pl.BlockSpec((1, KV_RANK), lambda i, j: (0, 0)),
                  pl.BlockSpec((KV_RANK, tn), lambda i, j: (0, j)),
                  pl.BlockSpec((tm, LANES), row),
                  pl.BlockSpec((tm, LANES), row),
                  pl.BlockSpec((tm, LANES), row)],
        out_specs=[pl.BlockSpec((hpb, tm, MLA_QK_PAD), lambda i, j: (j, i, 0)),
                   pl.BlockSpec((hpb, tm, MLA_V), lambda i, j: (j, i, 0))],
        out_shape=[jax.ShapeDtypeStruct((MLA_HEADS, S, MLA_QK_PAD), BF16),
                   jax.ShapeDtypeStruct((MLA_HEADS, S, MLA_V), BF16)],
        scratch_shapes=[pltpu.VMEM((tm, KV_RANK), BF16)],
        compiler_params=_cparams(2),
        name="mla_kv_proj",
    )(proj, proj, g.reshape(1, KV_RANK), w, cosf, slo, shi)


ATTN_HEADS_PER_STEP = 2
ATTN_EXP2_SCALE = float(MLA_NOPE + MLA_ROPE) ** -0.5 * 1.4426950408889634
ATTN_STRIP = 64
ATTN_LOG2_STEPS_PER_TRIP = 3


def _attn_body(q_ref, k_ref, v_ref, z_ref, o_ref, s_scr, p_scr, m_scr, a_scr, acc_scr, *, blk):
    i = pl.program_id(1)
    A, B = 0, 1
    ones_col = (lax.broadcasted_iota(jnp.int32, (blk, MLA_V), 1) == 0).astype(BF16)
    col_chunk = lax.broadcasted_iota(jnp.int32, (ATTN_STRIP, blk), 1) // CHUNK
    row_in_strip = lax.broadcasted_iota(jnp.int32, (ATTN_STRIP, blk), 0)

    def score(h, kb):
        k = k_ref[h, pl.ds(pl.multiple_of(kb * blk, blk), blk), :]
        s_scr[h] = lax.dot_general(q_ref[h], k, (((1,), (1,)), ((), ())), preferred_element_type=F32)

    def softmax(h, diagonal):
        for r in range(blk // ATTN_STRIP):
            rows = slice(r * ATTN_STRIP, (r + 1) * ATTN_STRIP)
            s = s_scr[h, rows, :]
            if diagonal:
                s = jnp.where(col_chunk <= (row_in_strip + r * ATTN_STRIP) // CHUNK, s, NEG)
                m_new = jnp.broadcast_to(jnp.max(s, axis=-1, keepdims=True), (ATTN_STRIP, LANES))
                a_scr[h, rows, :] = jnp.ones((ATTN_STRIP, LANES), F32)
            else:
                m_old = m_scr[h, rows, :]
                m_new = jnp.maximum(m_old, jnp.max(s, axis=-1, keepdims=True))
                a_scr[h, rows, :] = jnp.exp2(m_old - m_new)
            m_scr[h, rows, :] = m_new
            p_scr[h, rows, :] = jnp.exp2(s - jnp.concatenate([m_new] * (blk // LANES), axis=1)).astype(BF16)

    def apply_values(h, kb):
        v = v_ref[h, pl.ds(pl.multiple_of(kb * blk, blk), blk), :]
        pv = jnp.dot(p_scr[h], jnp.concatenate([v, ones_col], axis=1), preferred_element_type=F32)
        a = a_scr[h]
        acc_scr[h] = jnp.concatenate([a, a], axis=1) * acc_scr[h] + pv

    acc_scr[...] = jnp.zeros_like(acc_scr)
    score(A, i)
    softmax(A, True)
    score(B, i)
    softmax(B, True)
    score(A, 0)
    apply_values(A, i)

    def step(j):
        softmax(A, False)
        score(B, j - 1)
        apply_values(B, jnp.where(j == 1, i, j - 2))
        softmax(B, False)
        score(A, j)
        apply_values(A, j - 1)

    full_trips = lax.shift_right_logical(i, ATTN_LOG2_STEPS_PER_TRIP)
    per_trip = 1 << ATTN_LOG2_STEPS_PER_TRIP

    def trip(t, carry):
        for n in range(per_trip):
            step(per_trip * t + 1 + n)
        return carry

    lax.fori_loop(0, full_trips, trip, 0)

    done = per_trip * full_trips
    for bit in reversed(range(ATTN_LOG2_STEPS_PER_TRIP)):
        group = 1 << bit
        has_group = lax.bitwise_and(i, group) != 0

        @pl.when(has_group)
        def _(done=done, group=group):
            for n in range(group):
                step(done + 1 + n)

        done = done + jnp.where(has_group, group, 0)

    apply_values(B, jnp.maximum(i - 1, 0))
    for h in (A, B):
        cols = slice(h * MLA_V, (h + 1) * MLA_V)
        attn = acc_scr[h, :, 0:MLA_V] / acc_scr[h, :, MLA_V:MLA_V + 1]
        o_ref[:, cols] = (attn * _silu(z_ref[:, cols])).astype(o_ref.dtype)


def _attention(q, k, v, proj):
    H, S, _ = q.shape
    blk = min(512, S)
    hp = ATTN_HEADS_PER_STEP
    assert hp == 2
    return pl.pallas_call(
        functools.partial(_attn_body, blk=blk),
        grid=(H // hp, S // blk),
        in_specs=[pl.BlockSpec((hp, blk, MLA_QK_PAD), lambda h, i: (h, i, 0)),
                  pl.BlockSpec((hp, S, MLA_QK_PAD), lambda h, i: (h, 0, 0)),
                  pl.BlockSpec((hp, S, MLA_V), lambda h, i: (h, 0, 0)),
                  pl.BlockSpec((blk, hp * MLA_V), lambda h, i: (i, COL_ZA // (hp * MLA_V) + h))],
        out_specs=pl.BlockSpec((blk, hp * MLA_V), lambda h, i: (i, h)),
        out_shape=jax.ShapeDtypeStruct((S, MLA_WIDTH), BF16),
        scratch_shapes=[pltpu.VMEM((hp, blk, blk), F32), pltpu.VMEM((hp, blk, blk), BF16),
                        pltpu.VMEM((hp, blk, LANES), F32), pltpu.VMEM((hp, blk, LANES), F32),
                        pltpu.VMEM((hp, blk, 2 * MLA_V), F32)],
        compiler_params=_cparams(2),
        name="mla_attention",
    )(q, k, v, proj)


def _sgu_body(u_ref, v_ref, z_ref, lng_ref, lnb_ref, ws_ref, bst_ref, o_ref):
    v = v_ref[...]
    mu = jnp.mean(v, axis=-1, keepdims=True)
    vc = v - mu
    var = jnp.mean(vc * vc, axis=-1, keepdims=True)
    vn = (vc * lax.rsqrt(var + EPS) * lng_ref[...] + lnb_ref[...]).astype(BF16)
    out_chunk = lax.broadcasted_iota(jnp.int32, (SG_BLOCK, SG_BLOCK), 0) // CHUNK
    in_chunk = lax.broadcasted_iota(jnp.int32, (SG_BLOCK, SG_BLOCK), 1) // CHUNK
    causal = in_chunk <= out_chunk
    bst = bst_ref[...]
    for g in range(SG_GROUPS):
        w = jnp.where(causal, ws_ref[g], 0.0).astype(BF16)
        cols = slice(g * SG_GROUP_DIM, (g + 1) * SG_GROUP_DIM)
        bias = bst[:, g:g + 1]
        for b in range(u_ref.shape[0] // SG_BLOCK):
            rows = slice(b * SG_BLOCK, (b + 1) * SG_BLOCK)
            s = jnp.dot(w, vn[rows, cols], preferred_element_type=F32) + bias
            o_ref[rows, cols] = (u_ref[rows, cols] * s * _silu(z_ref[rows, cols])).astype(o_ref.dtype)


def _spatial_gating(proj, ln_g, ln_b, w_s, b_s):
    S = proj.shape[0]
    tm = min(512, S)
    blk = lambda c: pl.BlockSpec((tm, SG_WIDTH), lambda i: (i, c // SG_WIDTH))
    full = lambda shape: pl.BlockSpec(shape, lambda i: (0,) * len(shape))
    return pl.pallas_call(
        _sgu_body,
        grid=(S // tm,),
        in_specs=[blk(COL_SGU), blk(COL_SGV), blk(COL_ZB),
                  full((1, SG_WIDTH)), full((1, SG_WIDTH)),
                  full((SG_GROUPS, SG_BLOCK, SG_BLOCK)), full((SG_BLOCK, SG_GROUPS))],
        out_specs=pl.BlockSpec((tm, SG_WIDTH), lambda i: (i, 0)),
        out_shape=jax.ShapeDtypeStruct((S, SG_WIDTH), BF16),
        compiler_params=_cparams(1),
        name="spatial_gating",
    )(proj, proj, proj, ln_g.reshape(1, SG_WIDTH), ln_b.reshape(1, SG_WIDTH), w_s, jnp.transpose(b_s))


def _mlstm_body(x_ref, v_ref, og_ref, z_ref, if_ref, cw_ref, cb_ref, wq_ref, wk_ref,
                bi_ref, bf_ref, gn_ref, o_ref, xs_scr, ct_scr, n_scr, m_scr, h_scr, *, T, L):
    halo = 8

    @pl.when(pl.program_id(0) == 0)
    def _():
        xs_scr[0:halo, :] = jnp.zeros((halo, ML_WIDTH), F32)
        ct_scr[...] = jnp.zeros_like(ct_scr)
        n_scr[...] = jnp.zeros_like(n_scr)
        m_scr[...] = jnp.zeros_like(m_scr)

    xs_scr[halo:halo + T, :] = x_ref[...]
    cw = cw_ref[...]
    xc = cb_ref[...]
    for kk in range(CONV_K):
        off = halo - (CONV_K - 1) + kk
        xc = xc + cw[kk:kk + 1, :] * xs_scr[off:off + T, :]
    xs_scr[0:halo, :] = xs_scr[T:T + halo, :]
    xcb = _silu(xc).astype(BF16)

    gates = if_ref[...]
    head_lane = lax.broadcasted_iota(jnp.int32, gates.shape, 1) < ML_HEADS
    ig = jnp.where(head_lane, gates + bi_ref[...], 0.0)
    fg = jnp.where(head_lane, pltpu.roll(gates, LANES - ML_HEADS, 1) + bf_ref[...], 0.0)
    lf = jnp.minimum(fg, 0.0) - jnp.log1p(jnp.exp(-jnp.abs(fg)))
    rr = lax.broadcasted_iota(jnp.int32, (L, L), 0)
    cc = lax.broadcasted_iota(jnp.int32, (L, L), 1)
    tril = cc <= rr
    tri_f = tril.astype(F32)
    nc = T // L
    g_l, G_l, a_l, e_l, r_l = [], [], [], [], []
    for c in range(nc):
        rows = slice(c * L, (c + 1) * L)
        g_c = jnp.dot(tri_f, lf[rows], precision=lax.Precision.HIGHEST, preferred_element_type=F32)
        G_c = g_c[L - 1:L, :]
        w_end = G_c - g_c + ig[rows]
        a_c = jnp.max(w_end, axis=0, keepdims=True)
        g_l.append(g_c)
        G_l.append(G_c)
        a_l.append(a_c)
        e_l.append(jnp.exp(w_end - a_c))
        r_l.append(ig[rows] - g_c)
    r_t = jnp.transpose(jnp.concatenate(r_l, axis=0))

    q_l, k_l = [], []
    for h in range(ML_HEADS):
        cols = slice(h * ML_HEAD_DIM, (h + 1) * ML_HEAD_DIM)
        q_l.append(jnp.dot(xcb[:, cols], wq_ref[h], preferred_element_type=F32))
        k_l.append(jnp.dot(xcb[:, cols], wk_ref[h], preferred_element_type=F32) * (ML_HEAD_DIM ** -0.5))

    m_vec = m_scr[0:1, :]
    for c in range(nc):
        rows = slice(c * L, (c + 1) * L)
        g_c, G_c, a_c, e_c = g_l[c], G_l[c], a_l[c], e_l[c]
        m_new = jnp.maximum(G_c + m_vec, a_c)
        sp = jnp.exp(G_c + m_vec - m_new)
        sl = jnp.exp(a_c - m_new)
        for h in range(ML_HEADS):
            cols = slice(h * ML_HEAD_DIM, (h + 1) * ML_HEAD_DIM)
            q_c = q_l[h][rows]
            k_c = k_l[h][rows]
            v_c = v_ref[rows, cols]
            q_cb = q_c.astype(BF16)
            g_col = g_c[:, h:h + 1]
            d = jnp.where(tril, g_col + r_t[h:h + 1, rows], NEG)
            inter_log = g_col + m_vec[:, h:h + 1]
            m_i = jnp.maximum(jnp.max(d, axis=1, keepdims=True), inter_log)
            p = jnp.exp(d - m_i)
            qk = lax.dot_general(q_cb, k_c.astype(BF16), (((1,), (1,)), ((), ())),
                                 preferred_element_type=F32)
            sm = qk * p
            inter_scale = jnp.exp(inter_log - m_i)
            ct = ct_scr[h]
            n_prev = n_scr[h:h + 1, :]
            num = (jnp.dot(sm.astype(BF16), v_c.astype(BF16), preferred_element_type=F32)
                   + inter_scale * jnp.dot(q_cb, ct.astype(BF16), preferred_element_type=F32))
            den = (jnp.sum(sm, axis=1, keepdims=True)
                   + inter_scale * jnp.sum(q_c * n_prev, axis=1, keepdims=True))
            h_scr[rows, cols] = num / jnp.maximum(jnp.abs(den), jnp.exp(-m_i))

            e_col = e_c[:, h:h + 1]
            ev = (e_col * v_c).astype(BF16)
            c_loc_t = jnp.dot(jnp.transpose(k_c).astype(BF16), ev, preferred_element_type=F32)
            sp_h = sp[:, h:h + 1]
            sl_h = sl[:, h:h + 1]
            ct_scr[h] = sp_h * ct + sl_h * c_loc_t
            n_scr[h:h + 1, :] = sp_h * n_prev + sl_h * jnp.sum(e_col * k_c, axis=0, keepdims=True)
        m_vec = m_new
    m_scr[0:1, :] = m_vec

    for h in range(ML_HEADS):
        cols = slice(h * ML_HEAD_DIM, (h + 1) * ML_HEAD_DIM)
        hs = h_scr[:, cols] * _sigmoid(og_ref[:, cols])
        mu = jnp.mean(hs, axis=-1, keepdims=True)
        hc = hs - mu
        var = jnp.mean(hc * hc, axis=-1, keepdims=True)
        y = hc * lax.rsqrt(var + EPS) * gn_ref[:, cols]
        o_ref[:, cols] = (y * _silu(z_ref[:, cols])).astype(o_ref.dtype)


def _mlstm(proj, conv_w, conv_b, w_q, w_k, b_i, b_f, gn_g):
    S = proj.shape[0]
    T = min(512, S)
    L = ML_CHUNK
    wide = lambda c: pl.BlockSpec((T, ML_WIDTH), lambda i: (i, c // ML_WIDTH))
    gate = lambda c: pl.BlockSpec((T, LANES), lambda i: (i, c // LANES))
    full = lambda shape: pl.BlockSpec(shape, lambda i: (0,) * len(shape))
    pad_gate_bias = lambda b: jnp.pad(b, (0, LANES - ML_HEADS)).reshape(1, LANES)
    return pl.pallas_call(
        functools.partial(_mlstm_body, T=T, L=L),
        grid=(S // T,),
        in_specs=[wide(COL_MLX), wide(COL_MLV), wide(COL_MLO), wide(COL_ZC),
                  gate(COL_MLI),
                  full((CONV_K, ML_WIDTH)), full((1, ML_WIDTH)),
                  full((ML_HEADS, ML_HEAD_DIM, ML_HEAD_DIM)), full((ML_HEADS, ML_HEAD_DIM, ML_HEAD_DIM)),
                  full((1, LANES)), full((1, LANES)), full((1, ML_WIDTH))],
        out_specs=pl.BlockSpec((T, ML_WIDTH), lambda i: (i, 0)),
        out_shape=jax.ShapeDtypeStruct((S, ML_WIDTH), BF16),
        scratch_shapes=[pltpu.VMEM((T + 8, ML_WIDTH), F32),
                        pltpu.VMEM((ML_HEADS, ML_HEAD_DIM, ML_HEAD_DIM), F32),
                        pltpu.VMEM((8, ML_HEAD_DIM), F32),
                        pltpu.VMEM((8, LANES), F32),
                        pltpu.VMEM((T, ML_WIDTH), F32)],
        compiler_params=_cparams(1),
        name="mlstm",
    )(proj, proj, proj, proj, proj, conv_w, conv_b.reshape(1, ML_WIDTH), w_q, w_k,
      pad_gate_bias(b_i), pad_gate_bias(b_f), gn_g.reshape(1, ML_WIDTH))


def _merge_body(ya_ref, yb_ref, yc_ref, wa_ref, wb_ref, wc_ref, g0_ref, g1_ref, g2_ref, o_ref, wa_s, wb_s, wc_s):
    @pl.when(pl.program_id(1) == 0)
    def _():
        wa_s[...] = wa_ref[...].astype(BF16)
        wb_s[...] = wb_ref[...].astype(BF16)
        wc_s[...] = wc_ref[...].astype(BF16)

    pa = jnp.dot(ya_ref[...], wa_s[...], preferred_element_type=F32)
    pb = jnp.dot(yb_ref[...], wb_s[...], preferred_element_type=F32)
    pc = jnp.dot(yc_ref[...], wc_s[...], preferred_element_type=F32)
    merged = _sigmoid(g0_ref[...]) * pa + _sigmoid(g1_ref[...]) * pb + _sigmoid(g2_ref[...]) * pc
    o_ref[...] = merged.astype(o_ref.dtype)


def _merge(y_a, y_b, y_c, w_pa, w_pb, w_pc, layer, proj):
    S = y_a.shape[0]
    D = w_pa.shape[-1]
    tm = min(512, S)
    tn = 1024
    act = lambda width: pl.BlockSpec((tm, width), lambda j, i: (i, 0))
    wgt = lambda width: pl.BlockSpec((None, width, tn), lambda j, i: (layer, 0, j), pipeline_mode=pl.Buffered(1))
    gate = lambda b: pl.BlockSpec((tm, tn), lambda j, i: (i, (COL_GATE + b * D) // tn + j))
    return pl.pallas_call(
        _merge_body,
        grid=(D // tn, S // tm),
        in_specs=[act(MLA_WIDTH), act(SG_WIDTH), act(ML_WIDTH),
                  wgt(MLA_WIDTH), wgt(SG_WIDTH), wgt(ML_WIDTH),
                  gate(0), gate(1), gate(2)],
        out_specs=pl.BlockSpec((tm, tn), lambda j, i: (i, j)),
        out_shape=jax.ShapeDtypeStruct((S, D), BF16),
        scratch_shapes=[pltpu.VMEM((MLA_WIDTH, tn), BF16), pltpu.VMEM((SG_WIDTH, tn), BF16),
                        pltpu.VMEM((ML_WIDTH, tn), BF16)],
        compiler_params=_cparams(2),
        name="gated_merge",
    )(y_a, y_b, y_c, w_pa, w_pb, w_pc, proj, proj, proj)


def _outproj_body(m_ref, w_ref, x_ref, o_ref, w_s):
    @pl.when(pl.program_id(1) == 0)
    def _():
        w_s[...] = w_ref[...].astype(BF16)

    o_ref[...] = x_ref[...] + jnp.dot(m_ref[...], w_s[...], preferred_element_type=F32)


def _out_proj(merged, w_out, layer, x):
    S, D = x.shape
    tm = min(512, S)
    tn = 1024
    return pl.pallas_call(
        _outproj_body,
        grid=(D // tn, S // tm),
        in_specs=[pl.BlockSpec((tm, D), lambda j, i: (i, 0)),
                  pl.BlockSpec((None, D, tn), lambda j, i: (layer, 0, j), pipeline_mode=pl.Buffered(1)),
                  pl.BlockSpec((tm, tn), lambda j, i: (i, j))],
        out_specs=pl.BlockSpec((tm, tn), lambda j, i: (i, j)),
        out_shape=jax.ShapeDtypeStruct((S, D), F32),
        scratch_shapes=[pltpu.VMEM((D, tn), BF16)],
        compiler_params=_cparams(2),
        name="out_proj",
    )(merged, w_out, x)


def _pack_w_uq(w_uq):
    w = w_uq.reshape(Q_RANK, MLA_HEADS, MLA_NOPE + MLA_ROPE).astype(BF16)
    nope = w[:, :, :MLA_NOPE].reshape(Q_RANK, MLA_HEADS * MLA_NOPE)
    rope = w[:, :, MLA_NOPE:].reshape(Q_RANK, MLA_HEADS * MLA_ROPE)
    return jnp.concatenate([nope, rope], axis=1)


def _rope_tables(S):
    half = MLA_ROPE // 2
    inv_freq = ROPE_THETA ** (-jnp.arange(0, MLA_ROPE, 2, dtype=F32) / MLA_ROPE)
    ang = jnp.arange(S, dtype=F32)[:, None] * inv_freq[None, :]
    cos, sin = jnp.cos(ang), jnp.sin(ang)
    z = lambda n: jnp.zeros((S, n), F32)
    cosf = jnp.concatenate([cos, cos, z(LANES - MLA_ROPE)], axis=1)
    sin_lo = jnp.concatenate([-sin, z(LANES - half)], axis=1)
    sin_hi = jnp.concatenate([z(half), sin, z(LANES - MLA_ROPE)], axis=1)
    return cosf, sin_lo, sin_hi


def _layer(x, tables, layer, norm_g, w_in_t, mla_gq, mla_gkv, w_uq_p, w_ukv_b, sg_ln_g, sg_ln_b, sg_ws, sg_bs,
           ml_conv_w, ml_conv_b, ml_wq_b, ml_wk_b, ml_bi, ml_bf, ml_gn_g, w_pa, w_pb, w_pc, w_out):
    h = _rmsnorm(x, norm_g, BF16)
    proj = _in_proj(h, w_in_t, layer)
    q = _q_proj(proj, mla_gq, w_uq_p, *tables)
    k, v = _kv_proj(proj, mla_gkv, w_ukv_b, *tables)
    y_a = _attention(q, k, v, proj)
    y_b = _spatial_gating(proj, sg_ln_g, sg_ln_b, sg_ws, sg_bs)
    y_c = _mlstm(proj, ml_conv_w, ml_conv_b, ml_wq_b, ml_wk_b, ml_bi, ml_bf, ml_gn_g)
    merged = _merge(y_a, y_b, y_c, w_pa, w_pb, w_pc, layer, proj)
    return _out_proj(merged, w_out, layer, x)


def kernel(x, norm_g, w_in, mla_gq, mla_gkv, mla_wuq, mla_wukv, sg_ln_g, sg_ln_b, sg_ws, sg_bs, ml_conv_w, ml_conv_b, ml_wq, ml_wk, ml_bi, ml_bf, ml_gn_g, w_pa, w_pb, w_pc, w_out, final_g):
    B, S, D = x.shape
    depth = w_in.shape[0]
    tables = _rope_tables(S)
    cast = lambda w: w.astype(BF16)
    w_in_t = jnp.swapaxes(w_in, 1, 2)
    outs = []
    for b in range(B):
        xb = x[b]
        for l in range(depth):
            xb = _layer(xb, tables, l, norm_g[l], w_in_t, mla_gq[l], mla_gkv[l],
                        _pack_w_uq(mla_wuq[l]), cast(mla_wukv[l]),
                        sg_ln_g[l], sg_ln_b[l], sg_ws[l], sg_bs[l], ml_conv_w[l], ml_conv_b[l],
                        cast(ml_wq[l]), cast(ml_wk[l]), ml_bi[l], ml_bf[l], ml_gn_g[l],
                        w_pa, w_pb, w_pc, w_out)
        outs.append(_rmsnorm(xb, final_g, x.dtype))
    return outs[0][None] if B == 1 else jnp.stack(outs, axis=0)
```

```python
import functools

import jax
import jax.numpy as jnp
from jax import lax
from jax.experimental import pallas as pl
from jax.experimental.pallas import tpu as pltpu

F32 = jnp.float32
BF16 = jnp.bfloat16

D_MODEL = 4096
CHUNK = 64
EPS = 1e-6

MLA_HEADS = 16
MLA_NOPE = 128
MLA_ROPE = 64
MLA_V = 128
Q_RANK = 1024
KV_RANK = 512
ROPE_THETA = 10000.0
MLA_WIDTH = MLA_HEADS * MLA_V
MLA_QK_PAD = 256

SG_BLOCK = 128
SG_GROUPS = 4
SG_WIDTH = 1024
SG_GROUP_DIM = SG_WIDTH // SG_GROUPS

ML_HEADS = 4
ML_WIDTH = 1024
ML_HEAD_DIM = 256
CONV_K = 4
ML_CHUNK = 128

LANES = 128
NEG = -1e30

COL_CQ = 0
COL_ZA = 1024
COL_SGU = 3072
COL_SGV = 4096
COL_ZB = 5120
COL_MLX = 6144
COL_MLV = 7168
COL_MLO = 8192
COL_ZC = 9216
COL_GATE = 10240
COL_CKV = 22528
COL_KR = 23040
COL_MLI = 23168
IN_PAD = 23552

VMEM_LIMIT = 56 * 1024 * 1024


def _cparams(n_axes, vmem=VMEM_LIMIT):
    return pltpu.CompilerParams(dimension_semantics=("arbitrary",) * n_axes,
                                vmem_limit_bytes=vmem)


def _sigmoid(x):
    return 1.0 / (1.0 + jnp.exp(-x))


def _silu(x):
    return x * _sigmoid(x)


def _rmsnorm_body(x_ref, g_ref, o_ref):
    x = x_ref[...]
    y = x * lax.rsqrt(jnp.mean(x * x, axis=-1, keepdims=True) + EPS)
    o_ref[...] = (y * g_ref[...]).astype(o_ref.dtype)


def _rmsnorm(x, g, out_dtype):
    S, D = x.shape
    tm = min(512, S)
    return pl.pallas_call(
        _rmsnorm_body,
        grid=(S // tm,),
        in_specs=[pl.BlockSpec((tm, D), lambda i: (i, 0)),
                  pl.BlockSpec((1, D), lambda i: (0, 0))],
        out_specs=pl.BlockSpec((tm, D), lambda i: (i, 0)),
        out_shape=jax.ShapeDtypeStruct((S, D), out_dtype),
        compiler_params=_cparams(1),
        name="rmsnorm",
    )(x, g.reshape(1, D))


IN_TN = 1024
IN_CHUNK = 128
IN_CHUNKS_PER_TILE = IN_TN // IN_CHUNK

_IN_TILE_SRC = (0, 1600, 2624, 3648, 4672, 5696, 6720, 7744, 8768, 9792) + tuple(
    10824 + IN_TN * t for t in range(3 * D_MODEL // IN_TN))
_IN_CHUNK_SRC = tuple(src + IN_CHUNK * c for src in _IN_TILE_SRC for c in range(IN_CHUNKS_PER_TILE)) + (
    1024, 1152, 1280, 1408, 1536, 10816, 0, 0)
assert len(_IN_CHUNK_SRC) * IN_CHUNK == IN_PAD


def _in_proj_body(src_ref, h_ref, wt_hbm, o_ref, stage, wb, sem, *, layer, chunks_per_step):
    j = pl.program_id(0)
    i = pl.program_id(1)
    slot = lax.rem(j, 2)

    def chunk_copy(tile, c, buf):
        row = pl.multiple_of(src_ref[tile * IN_CHUNKS_PER_TILE + c], 8)
        return pltpu.make_async_copy(wt_hbm.at[layer, pl.ds(row, IN_CHUNK), :], stage.at[buf], sem.at[buf])

    def start_fetch(tile, chunks):
        chunk_copy(tile, chunks[0], 0).start()

    def finish_fetch(tile, dst_slot, chunks):
        for n, c in enumerate(chunks):
            if n + 1 < len(chunks):
                chunk_copy(tile, chunks[n + 1], (n + 1) % 2).start()
            chunk_copy(tile, c, n % 2).wait()
            start = c * IN_CHUNK if isinstance(c, int) else pl.multiple_of(c * IN_CHUNK, IN_CHUNK)
            wb[dst_slot, pl.ds(start, IN_CHUNK), :] = stage[n % 2].astype(BF16)

    @pl.when(jnp.logical_and(j == 0, i == 0))
    def _():
        first_tile = list(range(IN_CHUNKS_PER_TILE))
        start_fetch(0, first_tile)
        finish_fetch(0, 0, first_tile)

    prefetch = jnp.logical_and(j + 1 < pl.num_programs(0), i * chunks_per_step < IN_CHUNKS_PER_TILE)
    share = [i * chunks_per_step + n for n in range(chunks_per_step)]

    @pl.when(prefetch)
    def _():
        start_fetch(j + 1, share)

    o_ref[...] = lax.dot_general(h_ref[...], wb[slot], (((1,), (1,)), ((), ())), preferred_element_type=F32)

    @pl.when(prefetch)
    def _():
        finish_fetch(j + 1, 1 - slot, share)


def _in_proj(h, w_in_t, layer):
    S, K = h.shape
    tm = min(1024, S)
    n_row_tiles = S // tm
    chunks_per_step = -(-IN_CHUNKS_PER_TILE // n_row_tiles)
    assert IN_CHUNKS_PER_TILE % chunks_per_step == 0
    grid_spec = pltpu.PrefetchScalarGridSpec(
        num_scalar_prefetch=1,
        grid=(IN_PAD // IN_TN, n_row_tiles),
        in_specs=[pl.BlockSpec((tm, K), lambda j, i, src: (i, 0)),
                  pl.BlockSpec(memory_space=pl.ANY)],
        out_specs=pl.BlockSpec((tm, IN_TN), lambda j, i, src: (i, j)),
        scratch_shapes=[pltpu.VMEM((2, IN_CHUNK, K), F32), pltpu.VMEM((2, IN_TN, K), BF16),
                        pltpu.SemaphoreType.DMA((2,))])
    return pl.pallas_call(
        functools.partial(_in_proj_body, layer=layer, chunks_per_step=chunks_per_step),
        grid_spec=grid_spec,
        out_shape=jax.ShapeDtypeStruct((S, IN_PAD), F32),
        compiler_params=_cparams(2),
        name="in_proj",
    )(jnp.asarray(_IN_CHUNK_SRC, jnp.int32), h, w_in_t)


PROJ_ROW_PARTS = 2


def _rope(seg, cosf, sin_lo, sin_hi):
    return (seg * cosf
            + pltpu.roll(seg, LANES - MLA_ROPE // 2, 1) * sin_lo
            + pltpu.roll(seg, MLA_ROPE // 2, 1) * sin_hi)


def _qproj_body(cq_ref, g_ref, w_ref, cos_ref, slo_ref, shi_ref, o_ref, a_scr, wn_scr, wr_scr):
    j = pl.program_id(1)
    heads = o_ref.shape[0]

    @pl.when(jnp.logical_and(pl.program_id(0) == 0, j == 0))
    def _():
        per_head = MLA_NOPE + MLA_ROPE
        for h in range(MLA_HEADS):
            wn_scr[:, h * MLA_NOPE:(h + 1) * MLA_NOPE] = w_ref[:, h * per_head:h * per_head + MLA_NOPE].astype(BF16)
            wr_scr[:, h * MLA_ROPE:(h + 1) * MLA_ROPE] = w_ref[:, h * per_head + MLA_NOPE:(h + 1) * per_head].astype(BF16)

    @pl.when(j == 0)
    def _():
        c = cq_ref[...]
        y = c * lax.rsqrt(jnp.mean(c * c, axis=-1, keepdims=True) + EPS)
        a_scr[...] = (y * g_ref[...]).astype(BF16)

    wn = wn_scr[:, pl.ds(pl.multiple_of(j * heads * MLA_NOPE, heads * MLA_NOPE), heads * MLA_NOPE)]
    wr = wr_scr[:, pl.ds(pl.multiple_of(j * heads * MLA_ROPE, heads * MLA_ROPE), heads * MLA_ROPE)]
    part = a_scr.shape[0] // PROJ_ROW_PARTS
    for r in range(PROJ_ROW_PARTS):
        rows = slice(r * part, (r + 1) * part)
        nope = jnp.dot(a_scr[rows, :], wn, preferred_element_type=F32) * ATTN_EXP2_SCALE
        rope = jnp.dot(a_scr[rows, :], wr, preferred_element_type=F32) * ATTN_EXP2_SCALE
        cosf, slo, shi = cos_ref[rows, :], slo_ref[rows, :], shi_ref[rows, :]
        for hh in range(heads):
            o_ref[hh, rows, 0:MLA_NOPE] = nope[:, hh * MLA_NOPE:(hh + 1) * MLA_NOPE].astype(BF16)
            pair = rope[:, (hh // 2) * LANES:(hh // 2 + 1) * LANES]
            seg = pair if hh % 2 == 0 else pltpu.roll(pair, MLA_ROPE, 1)
            o_ref[hh, rows, MLA_NOPE:MLA_QK_PAD] = _rope(seg, cosf, slo, shi).astype(BF16)


def _q_proj(proj, g, w_uq, layer, cosf, slo, shi):
    S = proj.shape[0]
    tm = min(1024, S)
    hpb = 8
    n_out = MLA_HEADS * (MLA_NOPE + MLA_ROPE)
    row = lambda i, j: (i, 0)
    return pl.pallas_call(
        _qproj_body,
        grid=(S // tm, MLA_HEADS // hpb),
        in_specs=[pl.BlockSpec((tm, Q_RANK), lambda i, j: (i, COL_CQ // Q_RANK)),
                  pl.BlockSpec((1, Q_RANK), lambda i, j: (0, 0)),
                  pl.BlockSpec((None, Q_RANK, n_out), lambda i, j: (layer, 0, 0), pipeline_mode=pl.Buffered(1)),
                  pl.BlockSpec((tm, LANES), row),
                  pl.BlockSpec((tm, LANES), row),
                  pl.BlockSpec((tm, LANES), row)],
        out_specs=pl.BlockSpec((hpb, tm, MLA_QK_PAD), lambda i, j: (j, i, 0)),
        out_shape=jax.ShapeDtypeStruct((MLA_HEADS, S, MLA_QK_PAD), BF16),
        scratch_shapes=[pltpu.VMEM((tm, Q_RANK), BF16),
                        pltpu.VMEM((Q_RANK, MLA_HEADS * MLA_NOPE), BF16),
                        pltpu.VMEM((Q_RANK, MLA_HEADS * MLA_ROPE), BF16)],
        compiler_params=_cparams(2),
        name="mla_q_proj",
    )(proj, g.reshape(1, Q_RANK), w_uq, cosf, slo, shi)


def _kvproj_body(ckv_ref, kr_ref, g_ref, w_ref, cos_ref, slo_ref, shi_ref, k_ref, v_ref, a_scr):
    @pl.when(pl.program_id(1) == 0)
    def _():
        c = ckv_ref[...]
        y = c * lax.rsqrt(jnp.mean(c * c, axis=-1, keepdims=True) + EPS)
        a_scr[...] = (y * g_ref[...]).astype(BF16)

    width = MLA_NOPE + MLA_V
    part = a_scr.shape[0] // PROJ_ROW_PARTS
    for r in range(PROJ_ROW_PARTS):
        rows = slice(r * part, (r + 1) * part)
        res = jnp.dot(a_scr[rows, :], w_ref[...], preferred_element_type=F32)
        k_rot = _rope(kr_ref[rows, :], cos_ref[rows, :], slo_ref[rows, :], shi_ref[rows, :]).astype(BF16)
        for hh in range(k_ref.shape[0]):
            base = hh * width
            k_ref[hh, rows, 0:MLA_NOPE] = res[:, base:base + MLA_NOPE].astype(BF16)
            k_ref[hh, rows, MLA_NOPE:MLA_QK_PAD] = k_rot
            v_ref[hh, rows, :] = res[:, base + MLA_NOPE:base + width].astype(BF16)


def _kv_proj(proj, g, w, cosf, slo, shi):
    S = proj.shape[0]
    tm = min(1024, S)
    hpb = 8
    tn = hpb * (MLA_NOPE + MLA_V)
    row = lambda i, j: (i, 0)
    return pl.pallas_call(
        _kvproj_body,
        grid=(S // tm, MLA_HEADS // hpb),
        in_specs=[pl.BlockSpec((tm, KV_RANK), lambda i, j: (i, COL_CKV // KV_RANK)),
                  pl.BlockSpec((tm, LANES), lambda i, j: (i, COL_KR // LANES)),
                  pl.BlockSpec((1, KV_RANK), lambda i, j: (0, 0)),
                  pl.BlockSpec((KV_RANK, tn), lambda i, j: (0, j)),
                  pl.BlockSpec((tm, LANES), row),
                  pl.BlockSpec((tm, LANES), row),
                  pl.BlockSpec((tm, LANES), row)],
        out_specs=[pl.BlockSpec((hpb, tm, MLA_QK_PAD), lambda i, j: (j, i, 0)),
                   pl.BlockSpec((hpb, tm, MLA_V), lambda i, j: (j, i, 0))],
        out_shape=[jax.ShapeDtypeStruct((MLA_HEADS, S, MLA_QK_PAD), BF16),
                   jax.ShapeDtypeStruct((MLA_HEADS, S, MLA_V), BF16)],
        scratch_shapes=[pltpu.VMEM((tm, KV_RANK), BF16)],
        compiler_params=_cparams(2),
        name="mla_kv_proj",
    )(proj, proj, g.reshape(1, KV_RANK), w, cosf, slo, shi)


ATTN_HEADS_PER_STEP = 2
ATTN_EXP2_SCALE = float(MLA_NOPE + MLA_ROPE) ** -0.5 * 1.4426950408889634
ATTN_STRIP = 64
ATTN_LOG2_STEPS_PER_TRIP = 3


def _attn_body(q_ref, k_ref, v_ref, z_ref, o_ref, s_scr, p_scr, m_scr, a_scr, acc_scr, *, blk):
    i = pl.program_id(1)
    A, B = 0, 1
    ones_col = (lax.broadcasted_iota(jnp.int32, (blk, MLA_V), 1) == 0).astype(BF16)
    col_chunk = lax.broadcasted_iota(jnp.int32, (ATTN_STRIP, blk), 1) // CHUNK
    row_in_strip = lax.broadcasted_iota(jnp.int32, (ATTN_STRIP, blk), 0)

    def score(h, kb):
        k = k_ref[h, pl.ds(pl.multiple_of(kb * blk, blk), blk), :]
        s_scr[h] = lax.dot_general(q_ref[h], k, (((1,), (1,)), ((), ())), preferred_element_type=F32)

    def softmax(h, diagonal):
        for r in range(blk // ATTN_STRIP):
            rows = slice(r * ATTN_STRIP, (r + 1) * ATTN_STRIP)
            s = s_scr[h, rows, :]
            if diagonal:
                s = jnp.where(col_chunk <= (row_in_strip + r * ATTN_STRIP) // CHUNK, s, NEG)
                m_new = jnp.broadcast_to(jnp.max(s, axis=-1, keepdims=True), (ATTN_STRIP, LANES))
                a_scr[h, rows, :] = jnp.ones((ATTN_STRIP, LANES), F32)
            else:
                m_old = m_scr[h, rows, :]
                m_new = jnp.maximum(m_old, jnp.max(s, axis=-1, keepdims=True))
                a_scr[h, rows, :] = jnp.exp2(m_old - m_new)
            m_scr[h, rows, :] = m_new
            p_scr[h, rows, :] = jnp.exp2(s - jnp.concatenate([m_new] * (blk // LANES), axis=1)).astype(BF16)

    def apply_values(h, kb):
        v = v_ref[h, pl.ds(pl.multiple_of(kb * blk, blk), blk), :]
        pv = jnp.dot(p_scr[h], jnp.concatenate([v, ones_col], axis=1), preferred_element_type=F32)
        a = a_scr[h]
        acc_scr[h] = jnp.concatenate([a, a], axis=1) * acc_scr[h] + pv

    acc_scr[...] = jnp.zeros_like(acc_scr)
    score(A, i)
    softmax(A, True)
    score(B, i)
    softmax(B, True)
    score(A, 0)
    apply_values(A, i)

    def step(j):
        softmax(A, False)
        score(B, j - 1)
        apply_values(B, jnp.where(j == 1, i, j - 2))
        softmax(B, False)
        score(A, j)
        apply_values(A, j - 1)

    full_trips = lax.shift_right_logical(i, ATTN_LOG2_STEPS_PER_TRIP)
    per_trip = 1 << ATTN_LOG2_STEPS_PER_TRIP

    def trip(t, carry):
        for n in range(per_trip):
            step(per_trip * t + 1 + n)
        return carry

    lax.fori_loop(0, full_trips, trip, 0)

    done = per_trip * full_trips
    for bit in reversed(range(ATTN_LOG2_STEPS_PER_TRIP)):
        group = 1 << bit
        has_group = lax.bitwise_and(i, group) != 0

        @pl.when(has_group)
        def _(done=done, group=group):
            for n in range(group):
                step(done + 1 + n)

        done = done + jnp.where(has_group, group, 0)

    apply_values(B, jnp.maximum(i - 1, 0))
    for h in (A, B):
        cols = slice(h * MLA_V, (h + 1) * MLA_V)
        attn = acc_scr[h, :, 0:MLA_V] / acc_scr[h, :, MLA_V:MLA_V + 1]
        o_ref[:, cols] = (attn * _silu(z_ref[:, cols])).astype(o_ref.dtype)


def _attention(q, k, v, proj):
    H, S, _ = q.shape
    blk = min(512, S)
    hp = ATTN_HEADS_PER_STEP
    assert hp == 2
    return pl.pallas_call(
        functools.partial(_attn_body, blk=blk),
        grid=(H // hp, S // blk),
        in_specs=[pl.BlockSpec((hp, blk, MLA_QK_PAD), lambda h, i: (h, i, 0)),
                  pl.BlockSpec((hp, S, MLA_QK_PAD), lambda h, i: (h, 0, 0)),
                  pl.BlockSpec((hp, S, MLA_V), lambda h, i: (h, 0, 0)),
                  pl.BlockSpec((blk, hp * MLA_V), lambda h, i: (i, COL_ZA // (hp * MLA_V) + h))],
        out_specs=pl.BlockSpec((blk, hp * MLA_V), lambda h, i: (i, h)),
        out_shape=jax.ShapeDtypeStruct((S, MLA_WIDTH), BF16),
        scratch_shapes=[pltpu.VMEM((hp, blk, blk), F32), pltpu.VMEM((hp, blk, blk), BF16),
                        pltpu.VMEM((hp, blk, LANES), F32), pltpu.VMEM((hp, blk, LANES), F32),
                        pltpu.VMEM((hp, blk, 2 * MLA_V), F32)],
        compiler_params=_cparams(2),
        name="mla_attention",
    )(q, k, v, proj)


def _sgu_body(u_ref, v_ref, z_ref, lng_ref, lnb_ref, ws_ref, bst_ref, o_ref):
    v = v_ref[...]
    mu = jnp.mean(v, axis=-1, keepdims=True)
    vc = v - mu
    var = jnp.mean(vc * vc, axis=-1, keepdims=True)
    vn = (vc * lax.rsqrt(var + EPS) * lng_ref[...] + lnb_ref[...]).astype(BF16)
    out_chunk = lax.broadcasted_iota(jnp.int32, (SG_BLOCK, SG_BLOCK), 0) // CHUNK
    in_chunk = lax.broadcasted_iota(jnp.int32, (SG_BLOCK, SG_BLOCK), 1) // CHUNK
    causal = in_chunk <= out_chunk
    bst = bst_ref[...]
    for g in range(SG_GROUPS):
        w = jnp.where(causal, ws_ref[g], 0.0).astype(BF16)
        cols = slice(g * SG_GROUP_DIM, (g + 1) * SG_GROUP_DIM)
        bias = bst[:, g:g + 1]
        for b in range(u_ref.shape[0] // SG_BLOCK):
            rows = slice(b * SG_BLOCK, (b + 1) * SG_BLOCK)
            s = jnp.dot(w, vn[rows, cols], preferred_element_type=F32) + bias
            o_ref[rows, cols] = (u_ref[rows, cols] * s * _silu(z_ref[rows, cols])).astype(o_ref.dtype)


def _spatial_gating(proj, ln_g, ln_b, w_s, b_s):
    S = proj.shape[0]
    tm = min(512, S)
    blk = lambda c: pl.BlockSpec((tm, SG_WIDTH), lambda i: (i, c // SG_WIDTH))
    full = lambda shape: pl.BlockSpec(shape, lambda i: (0,) * len(shape))
    return pl.pallas_call(
        _sgu_body,
        grid=(S // tm,),
        in_specs=[blk(COL_SGU), blk(COL_SGV), blk(COL_ZB),
                  full((1, SG_WIDTH)), full((1, SG_WIDTH)),
                  full((SG_GROUPS, SG_BLOCK, SG_BLOCK)), full((SG_BLOCK, SG_GROUPS))],
        out_specs=pl.BlockSpec((tm, SG_WIDTH), lambda i: (i, 0)),
        out_shape=jax.ShapeDtypeStruct((S, SG_WIDTH), BF16),
        compiler_params=_cparams(1),
        name="spatial_gating",
    )(proj, proj, proj, ln_g.reshape(1, SG_WIDTH), ln_b.reshape(1, SG_WIDTH), w_s, jnp.transpose(b_s))


def _mlstm_body(x_ref, v_ref, og_ref, z_ref, if_ref, cw_ref, cb_ref, wq_ref, wk_ref,
                bi_ref, bf_ref, gn_ref, o_ref, xs_scr, ct_scr, n_scr, m_scr, h_scr, *, T, L):
    halo = 8

    @pl.when(pl.program_id(0) == 0)
    def _():
        xs_scr[0:halo, :] = jnp.zeros((halo, ML_WIDTH), F32)
        ct_scr[...] = jnp.zeros_like(ct_scr)
        n_scr[...] = jnp.zeros_like(n_scr)
        m_scr[...] = jnp.zeros_like(m_scr)

    xs_scr[halo:halo + T, :] = x_ref[...]
    cw = cw_ref[...]
    xc = cb_ref[...]
    for kk in range(CONV_K):
        off = halo - (CONV_K - 1) + kk
        xc = xc + cw[kk:kk + 1, :] * xs_scr[off:off + T, :]
    xs_scr[0:halo, :] = xs_scr[T:T + halo, :]
    xcb = _silu(xc).astype(BF16)

    gates = if_ref[...]
    head_lane = lax.broadcasted_iota(jnp.int32, gates.shape, 1) < ML_HEADS
    ig = jnp.where(head_lane, gates + bi_ref[...], 0.0)
    fg = jnp.where(head_lane, pltpu.roll(gates, LANES - ML_HEADS, 1) + bf_ref[...], 0.0)
    lf = jnp.minimum(fg, 0.0) - jnp.log1p(jnp.exp(-jnp.abs(fg)))
    rr = lax.broadcasted_iota(jnp.int32, (L, L), 0)
    cc = lax.broadcasted_iota(jnp.int32, (L, L), 1)
    tril = cc <= rr
    tri_f = tril.astype(F32)
    nc = T // L
    g_l, G_l, a_l, e_l, r_l = [], [], [], [], []
    for c in range(nc):
        rows = slice(c * L, (c + 1) * L)
        g_c = jnp.dot(tri_f, lf[rows], precision=lax.Precision.HIGHEST, preferred_element_type=F32)
        G_c = g_c[L - 1:L, :]
        w_end = G_c - g_c + ig[rows]
        a_c = jnp.max(w_end, axis=0, keepdims=True)
        g_l.append(g_c)
        G_l.append(G_c)
        a_l.append(a_c)
        e_l.append(jnp.exp(w_end - a_c))
        r_l.append(ig[rows] - g_c)
    r_t = jnp.transpose(jnp.concatenate(r_l, axis=0))

    q_l, k_l = [], []
    for h in range(ML_HEADS):
        cols = slice(h * ML_HEAD_DIM, (h + 1) * ML_HEAD_DIM)
        q_l.append(jnp.dot(xcb[:, cols], wq_ref[h], preferred_element_type=F32))
        k_l.append(jnp.dot(xcb[:, cols], wk_ref[h], preferred_element_type=F32) * (ML_HEAD_DIM ** -0.5))

    m_vec = m_scr[0:1, :]
    for c in range(nc):
        rows = slice(c * L, (c + 1) * L)
        g_c, G_c, a_c, e_c = g_l[c], G_l[c], a_l[c], e_l[c]
        m_new = jnp.maximum(G_c + m_vec, a_c)
        sp = jnp.exp(G_c + m_vec - m_new)
        sl = jnp.exp(a_c - m_new)
        for h in range(ML_HEADS):
            cols = slice(h * ML_HEAD_DIM, (h + 1) * ML_HEAD_DIM)
            q_c = q_l[h][rows]
            k_c = k_l[h][rows]
            v_c = v_ref[rows, cols]
            q_cb = q_c.astype(BF16)
            g_col = g_c[:, h:h + 1]
            d = jnp.where(tril, g_col + r_t[h:h + 1, rows], NEG)
            inter_log = g_col + m_vec[:, h:h + 1]
            m_i = jnp.maximum(jnp.max(d, axis=1, keepdims=True), inter_log)
            p = jnp.exp(d - m_i)
            qk = lax.dot_general(q_cb, k_c.astype(BF16), (((1,), (1,)), ((), ())),
                                 preferred_element_type=F32)
            sm = qk * p
            inter_scale = jnp.exp(inter_log - m_i)
            ct = ct_scr[h]
            n_prev = n_scr[h:h + 1, :]
            num = (jnp.dot(sm.astype(BF16), v_c.astype(BF16), preferred_element_type=F32)
                   + inter_scale * jnp.dot(q_cb, ct.astype(BF16), preferred_element_type=F32))
            den = (jnp.sum(sm, axis=1, keepdims=True)
                   + inter_scale * jnp.sum(q_c * n_prev, axis=1, keepdims=True))
            h_scr[rows, cols] = num / jnp.maximum(jnp.abs(den), jnp.exp(-m_i))

            e_col = e_c[:, h:h + 1]
            ev = (e_col * v_c).astype(BF16)
            c_loc_t = jnp.dot(jnp.transpose(k_c).astype(BF16), ev, preferred_element_type=F32)
            sp_h = sp[:, h:h + 1]
            sl_h = sl[:, h:h + 1]
            ct_scr[h] = sp_h * ct + sl_h * c_loc_t
            n_scr[h:h + 1, :] = sp_h * n_prev + sl_h * jnp.sum(e_col * k_c, axis=0, keepdims=True)
        m_vec = m_new
    m_scr[0:1, :] = m_vec

    for h in range(ML_HEADS):
        cols = slice(h * ML_HEAD_DIM, (h + 1) * ML_HEAD_DIM)
        hs = h_scr[:, cols] * _sigmoid(og_ref[:, cols])
        mu = jnp.mean(hs, axis=-1, keepdims=True)
        hc = hs - mu
        var = jnp.mean(hc * hc, axis=-1, keepdims=True)
        y = hc * lax.rsqrt(var + EPS) * gn_ref[:, cols]
        o_ref[:, cols] = (y * _silu(z_ref[:, cols])).astype(o_ref.dtype)


def _mlstm(proj, conv_w, conv_b, w_q, w_k, b_i, b_f, gn_g):
    S = proj.shape[0]
    T = min(512, S)
    L = ML_CHUNK
    wide = lambda c: pl.BlockSpec((T, ML_WIDTH), lambda i: (i, c // ML_WIDTH))
    gate = lambda c: pl.BlockSpec((T, LANES), lambda i: (i, c // LANES))
    full = lambda shape: pl.BlockSpec(shape, lambda i: (0,) * len(shape))
    pad_gate_bias = lambda b: jnp.pad(b, (0, LANES - ML_HEADS)).reshape(1, LANES)
    return pl.pallas_call(
        functools.partial(_mlstm_body, T=T, L=L),
        grid=(S // T,),
        in_specs=[wide(COL_MLX), wide(COL_MLV), wide(COL_MLO), wide(COL_ZC),
                  gate(COL_MLI),
                  full((CONV_K, ML_WIDTH)), full((1, ML_WIDTH)),
                  full((ML_HEADS, ML_HEAD_DIM, ML_HEAD_DIM)), full((ML_HEADS, ML_HEAD_DIM, ML_HEAD_DIM)),
                  full((1, LANES)), full((1, LANES)), full((1, ML_WIDTH))],
        out_specs=pl.BlockSpec((T, ML_WIDTH), lambda i: (i, 0)),
        out_shape=jax.ShapeDtypeStruct((S, ML_WIDTH), BF16),
        scratch_shapes=[pltpu.VMEM((T + 8, ML_WIDTH), F32),
                        pltpu.VMEM((ML_HEADS, ML_HEAD_DIM, ML_HEAD_DIM), F32),
                        pltpu.VMEM((8, ML_HEAD_DIM), F32),
                        pltpu.VMEM((8, LANES), F32),
                        pltpu.VMEM((T, ML_WIDTH), F32)],
        compiler_params=_cparams(1),
        name="mlstm",
    )(proj, proj, proj, proj, proj, conv_w, conv_b.reshape(1, ML_WIDTH), w_q, w_k,
      pad_gate_bias(b_i), pad_gate_bias(b_f), gn_g.reshape(1, ML_WIDTH))


def _merge_body(ya_ref, yb_ref, yc_ref, wa_ref, wb_ref, wc_ref, g0_ref, g1_ref, g2_ref, o_ref, wa_s, wb_s, wc_s):
    @pl.when(pl.program_id(1) == 0)
    def _():
        wa_s[...] = wa_ref[...].astype(BF16)
        wb_s[...] = wb_ref[...].astype(BF16)
        wc_s[...] = wc_ref[...].astype(BF16)

    pa = jnp.dot(ya_ref[...], wa_s[...], preferred_element_type=F32)
    pb = jnp.dot(yb_ref[...], wb_s[...], preferred_element_type=F32)
    pc = jnp.dot(yc_ref[...], wc_s[...], preferred_element_type=F32)
    merged = _sigmoid(g0_ref[...]) * pa + _sigmoid(g1_ref[...]) * pb + _sigmoid(g2_ref[...]) * pc
    o_ref[...] = merged.astype(o_ref.dtype)


def _merge(y_a, y_b, y_c, w_pa, w_pb, w_pc, layer, proj):
    S = y_a.shape[0]
    D = w_pa.shape[-1]
    tm = min(512, S)
    tn = 1024
    act = lambda width: pl.BlockSpec((tm, width), lambda j, i: (i, 0))
    wgt = lambda width: pl.BlockSpec((None, width, tn), lambda j, i: (layer, 0, j), pipeline_mode=pl.Buffered(1))
    gate = lambda b: pl.BlockSpec((tm, tn), lambda j, i: (i, (COL_GATE + b * D) // tn + j))
    return pl.pallas_call(
        _merge_body,
        grid=(D // tn, S // tm),
        in_specs=[act(MLA_WIDTH), act(SG_WIDTH), act(ML_WIDTH),
                  wgt(MLA_WIDTH), wgt(SG_WIDTH), wgt(ML_WIDTH),
                  gate(0), gate(1), gate(2)],
        out_specs=pl.BlockSpec((tm, tn), lambda j, i: (i, j)),
        out_shape=jax.ShapeDtypeStruct((S, D), BF16),
        scratch_shapes=[pltpu.VMEM((MLA_WIDTH, tn), BF16), pltpu.VMEM((SG_WIDTH, tn), BF16),
                        pltpu.VMEM((ML_WIDTH, tn), BF16)],
        compiler_params=_cparams(2),
        name="gated_merge",
    )(y_a, y_b, y_c, w_pa, w_pb, w_pc, proj, proj, proj)


def _outproj_body(m_ref, w_ref, x_ref, o_ref, w_s):
    @pl.when(pl.program_id(1) == 0)
    def _():
        w_s[...] = w_ref[...].astype(BF16)

    o_ref[...] = x_ref[...] + jnp.dot(m_ref[...], w_s[...], preferred_element_type=F32)


def _out_proj(merged, w_out, layer, x):
    S, D = x.shape
    tm = min(512, S)
    tn = 1024
    return pl.pallas_call(
        _outproj_body,
        grid=(D // tn, S // tm),
        in_specs=[pl.BlockSpec((tm, D), lambda j, i: (i, 0)),
                  pl.BlockSpec((None, D, tn), lambda j, i: (layer, 0, j), pipeline_mode=pl.Buffered(1)),
                  pl.BlockSpec((tm, tn), lambda j, i: (i, j))],
        out_specs=pl.BlockSpec((tm, tn), lambda j, i: (i, j)),
        out_shape=jax.ShapeDtypeStruct((S, D), F32),
        scratch_shapes=[pltpu.VMEM((D, tn), BF16)],
        compiler_params=_cparams(2),
        name="out_proj",
    )(merged, w_out, x)


def _rope_tables(S):
    half = MLA_ROPE // 2
    inv_freq = ROPE_THETA ** (-jnp.arange(0, MLA_ROPE, 2, dtype=F32) / MLA_ROPE)
    ang = jnp.arange(S, dtype=F32)[:, None] * inv_freq[None, :]
    cos, sin = jnp.cos(ang), jnp.sin(ang)
    z = lambda n: jnp.zeros((S, n), F32)
    cosf = jnp.concatenate([cos, cos, z(LANES - MLA_ROPE)], axis=1)
    sin_lo = jnp.concatenate([-sin, z(LANES - half)], axis=1)
    sin_hi = jnp.concatenate([z(half), sin, z(LANES - MLA_ROPE)], axis=1)
    return cosf, sin_lo, sin_hi


def _layer(x, tables, layer, norm_g, w_in_t, mla_gq, mla_gkv, w_uq, w_ukv_b, sg_ln_g, sg_ln_b, sg_ws, sg_bs,
           ml_conv_w, ml_conv_b, ml_wq_b, ml_wk_b, ml_bi, ml_bf, ml_gn_g, w_pa, w_pb, w_pc, w_out):
    h = _rmsnorm(x, norm_g, BF16)
    proj = _in_proj(h, w_in_t, layer)
    q = _q_proj(proj, mla_gq, w_uq, layer, *tables)
    k, v = _kv_proj(proj, mla_gkv, w_ukv_b, *tables)
    y_a = _attention(q, k, v, proj)
    y_b = _spatial_gating(proj, sg_ln_g, sg_ln_b, sg_ws, sg_bs)
    y_c = _mlstm(proj, ml_conv_w, ml_conv_b, ml_wq_b, ml_wk_b, ml_bi, ml_bf, ml_gn_g)
    merged = _merge(y_a, y_b, y_c, w_pa, w_pb, w_pc, layer, proj)
    return _out_proj(merged, w_out, layer, x)


def kernel(x, norm_g, w_in, mla_gq, mla_gkv, mla_wuq, mla_wukv, sg_ln_g, sg_ln_b, sg_ws, sg_bs, ml_conv_w, ml_conv_b, ml_wq, ml_wk, ml_bi, ml_bf, ml_gn_g, w_pa, w_pb, w_pc, w_out, final_g):
    B, S, D = x.shape
    depth = w_in.shape[0]
    tables = _rope_tables(S)
    cast = lambda w: w.astype(BF16)
    w_in_t = jnp.swapaxes(w_in, 1, 2)
    outs = []
    for b in range(B):
        xb = x[b]
        for l in range(depth):
            xb = _layer(xb, tables, l, norm_g[l], w_in_t, mla_gq[l], mla_gkv[l],
                        mla_wuq, cast(mla_wukv[l]),
                        sg_ln_g[l], sg_ln_b[l], sg_ws[l], sg_bs[l], ml_conv_w[l], ml_conv_b[l],
                        cast(ml_wq[l]), cast(ml_wk[l]), ml_bi[l], ml_bf[l], ml_gn_g[l],
                        w_pa, w_pb, w_pc, w_out)
        outs.append(_rmsnorm(xb, final_g, x.dtype))
    return outs[0][None] if B == 1 else jnp.stack(outs, axis=0)
```
